```python
import jax
import jax.numpy as jnp
from jax import lax
import numpy as np

D_MODEL = 1024
BATCH = 1
SEQ = 16384
DEPTH = 2
DEC_BATCH = 32
DEC_SEQ = 4
PAST_LEN = 16384
PAGE_SIZE = 128

HEAD_DIM = 64
N_HEADS = D_MODEL // HEAD_DIM
N_HEADS_A = N_HEADS // 4
N_HEADS_B = N_HEADS // 4
N_HEADS_C = N_HEADS - N_HEADS_A - N_HEADS_B
HB0 = N_HEADS_A
HC0 = N_HEADS_A + N_HEADS_B
QKV_W = N_HEADS * HEAD_DIM
N_BRANCH = 3
IN_COLS = 3 * QKV_W + N_HEADS_B + N_BRANCH * D_MODEL
ROT_DIM = HEAD_DIM // 4
ROPE_THETA = 500000.0
SCALE = HEAD_DIM ** -0.5
Q_BLOCK = 128
MOBA_BLOCK = 256
MOBA_TOPK = 3
D_FF = 256 * ((8 * D_MODEL // 3 + 255) // 256)
N_EXPERTS = 8
TOP_K = 2
D_EXPERT = 7 * D_MODEL // 2
N_DENSE = (DEPTH + 1) // 2
N_MOE = DEPTH // 2
EPS = 1e-6
NEG = -1e30

kernel_name = 'hybrid_sb_fox_moba_decoder_step'


def _rmsnorm(x, g):
    xf = x.astype(jnp.float32)
    r = lax.rsqrt(jnp.mean(xf * xf, axis=-1, keepdims=True) + EPS)
    return (xf * r).astype(x.dtype) * g


def _partial_rope(x, pos):
    half = ROT_DIM // 2
    inv = ROPE_THETA ** (-jnp.arange(0, ROT_DIM, 2, dtype=jnp.float32) / ROT_DIM)
    ang = pos.astype(jnp.float32)[:, None] * inv[None, :]
    cos = jnp.cos(ang)[:, None, :]
    sin = jnp.sin(ang)[:, None, :]
    xr = x[..., :ROT_DIM].astype(jnp.float32)
    x1, x2 = xr[..., :half], xr[..., half:]
    rot = jnp.concatenate([x1 * cos - x2 * sin, x2 * cos + x1 * sin], axis=-1).astype(x.dtype)
    return jnp.concatenate([rot, x[..., ROT_DIM:]], axis=-1)


def _stick_breaking(z, mask):
    log_beta = jax.nn.log_sigmoid(z)
    log_1m = jnp.where(mask, jax.nn.log_sigmoid(-z), 0.0)
    after = lax.cumsum(log_1m, axis=z.ndim - 1, reverse=True) - log_1m
    return jnp.where(mask, jnp.exp(log_beta + after), 0.0)


def _project(h, w_in_l, b_f_l, pos):
    b, t, _ = h.shape
    proj = h @ w_in_l
    q = proj[..., :QKV_W].reshape(b, t, N_HEADS, HEAD_DIM)
    k = proj[..., QKV_W:2 * QKV_W].reshape(b, t, N_HEADS, HEAD_DIM)
    v = proj[..., 2 * QKV_W:3 * QKV_W].reshape(b, t, N_HEADS, HEAD_DIM)
    f_logit = proj[..., 3 * QKV_W:3 * QKV_W + N_HEADS_B] + b_f_l
    gates = proj[..., 3 * QKV_W + N_HEADS_B:].reshape(b, t, N_BRANCH, D_MODEL)
    q = jnp.concatenate([q[:, :, :HC0], _partial_rope(q[:, :, HC0:], pos)], axis=2)
    k = jnp.concatenate([k[:, :, :HC0], _partial_rope(k[:, :, HC0:], pos)], axis=2)
    logf = jax.nn.log_sigmoid(f_logit.astype(jnp.float32))
    return q, k, v, logf, gates


def _merge(ya, yb, yc, gates, w_br_a_l, w_br_b_l, w_br_c_l, w_out_l):
    b, t = ya.shape[:2]
    g = jax.nn.sigmoid(gates)
    m = (g[:, :, 0] * (ya.reshape(b, t, -1) @ w_br_a_l)
         + g[:, :, 1] * (yb.reshape(b, t, -1) @ w_br_b_l)
         + g[:, :, 2] * (yc.reshape(b, t, -1) @ w_br_c_l))
    return m @ w_out_l


def _prompt_mix(q, k, v, logf):
    f32 = jnp.float32
    b, t = q.shape[:2]
    qa, ka, va = q[:, :, :HB0], k[:, :, :HB0], v[:, :, :HB0]
    qb, kb, vb = q[:, :, HB0:HC0], k[:, :, HB0:HC0], v[:, :, HB0:HC0]
    qc, kc, vc = q[:, :, HC0:], k[:, :, HC0:], v[:, :, HC0:]
    kpos = jnp.arange(t)
    ccum = jnp.cumsum(logf, axis=1)
    ccum_k = ccum.transpose(0, 2, 1)
    n_blk = -(-t // MOBA_BLOCK)
    pad = n_blk * MOBA_BLOCK - t
    kc_pad = jnp.pad(kc, ((0, 0), (0, pad), (0, 0), (0, 0)))
    vc_pad = jnp.pad(vc, ((0, 0), (0, pad), (0, 0), (0, 0)))
    kc_bh = kc_pad.reshape(b, n_blk, MOBA_BLOCK, N_HEADS_C, HEAD_DIM).transpose(0, 3, 1, 2, 4)
    vc_bh = vc_pad.reshape(b, n_blk, MOBA_BLOCK, N_HEADS_C, HEAD_DIM).transpose(0, 3, 1, 2, 4)
    k_mean = jnp.mean(kc_bh.astype(f32), axis=3)
    topk = min(MOBA_TOPK, n_blk)
    n_sel = topk * MOBA_BLOCK
    bi = jnp.arange(b)[:, None, None, None]
    hi = jnp.arange(N_HEADS_C)[None, :, None, None]

    def one_block(i):
        q0 = i * Q_BLOCK
        qpos = q0 + jnp.arange(Q_BLOCK)

        def sl(a):
            return lax.dynamic_slice_in_dim(a, q0, Q_BLOCK, axis=1)

        z = jnp.einsum('bqhd,bkhd->bhqk', sl(qa), ka).astype(f32) * SCALE
        w = _stick_breaking(z, kpos[None, :] < qpos[:, None])
        ya = jnp.einsum('bhqk,bkhd->bqhd', w.astype(va.dtype), va)
        lg = (jnp.einsum('bqhd,bkhd->bhqk', sl(qb), kb).astype(f32) * SCALE
              + sl(ccum).transpose(0, 2, 1)[..., None] - ccum_k[:, :, None, :])
        lg = jnp.where(kpos[None, :] <= qpos[:, None], lg, NEG)
        p = jax.nn.softmax(lg, axis=-1)
        yb = jnp.einsum('bhqk,bkhd->bqhd', p.astype(vb.dtype), vb)
        qcb = sl(qc)
        own = q0 // MOBA_BLOCK
        gate = jnp.einsum('bqhd,bhnd->bhqn', qcb.astype(f32), k_mean)
        gate = jnp.where(jnp.arange(n_blk) < own, gate, NEG)
        _, sel = lax.top_k(gate, topk)
        sel_ok = (jnp.arange(topk) < own)[:, None]
        k_sel = kc_bh[bi, hi, sel]
        v_sel = vc_bh[bi, hi, sel]
        lg_sel = jnp.einsum('bqhd,bhqnkd->bhqnk', qcb, k_sel).astype(f32) * SCALE
        lg_sel = jnp.where(sel_ok, lg_sel, NEG).reshape(b, N_HEADS_C, Q_BLOCK, n_sel)
        k_own = lax.dynamic_slice_in_dim(kc_pad, own * MOBA_BLOCK, MOBA_BLOCK, axis=1)
        v_own = lax.dynamic_slice_in_dim(vc_pad, own * MOBA_BLOCK, MOBA_BLOCK, axis=1)
        opos = own * MOBA_BLOCK + jnp.arange(MOBA_BLOCK)
        lg_own = jnp.einsum('bqhd,bkhd->bhqk', qcb, k_own).astype(f32) * SCALE
        lg_own = jnp.where(opos[None, :] <= qpos[:, None], lg_own, NEG)
        pc = jax.nn.softmax(jnp.concatenate([lg_sel, lg_own], axis=-1), axis=-1).astype(vc.dtype)
        yc = (jnp.einsum('bhqk,bhqkd->bqhd', pc[..., :n_sel],
                         v_sel.reshape(b, N_HEADS_C, Q_BLOCK, n_sel, HEAD_DIM))
              + jnp.einsum('bhqk,bkhd->bqhd', pc[..., n_sel:], v_own))
        return ya, yb, yc

    ya, yb, yc = lax.map(one_block, jnp.arange(t // Q_BLOCK))

    def unblock(y):
        return y.transpose(1, 0, 2, 3, 4).reshape(b, t, y.shape[3], HEAD_DIM)

    return unblock(ya), unblock(yb), unblock(yc)


def _sample_moba(l, qc, kc, vc, qpos, cache_k, cache_v, page_table):
    f32 = jnp.float32
    n_b, n_s = qc.shape[:2]
    n_pages = page_table.shape[1]
    page = cache_k.shape[2]
    ppb = MOBA_BLOCK // page
    n_full = (n_pages * page) // MOBA_BLOCK
    own = qpos // MOBA_BLOCK
    lgs, vals, eqs = [], [], []
    if n_full > 0:
        k_rows = cache_k[l, page_table, :, HC0:].astype(f32)
        blk_sum = jnp.sum(k_rows, axis=2)[:, :n_full * ppb].reshape(
            n_b, n_full, ppb, N_HEADS_C, HEAD_DIM).sum(axis=2)
        k_mean = blk_sum.transpose(0, 2, 1, 3) / MOBA_BLOCK
        gate = jnp.einsum('bqhd,bhnd->bhqn', qc.astype(f32), k_mean)
        gate = jnp.where(jnp.arange(n_full)[None, :] < own[:, None], gate, NEG)
        topk = min(MOBA_TOPK, n_full)
        _, sel = lax.top_k(gate, topk)
        sel_ok = (jnp.arange(topk)[None, :] < own[:, None])[:, :, None]
        logical = sel[..., None] * ppb + jnp.arange(ppb)
        phys = page_table[jnp.arange(n_b)[:, None, None, None, None], logical]
        hidx = HC0 + jnp.arange(N_HEADS_C)[None, :, None, None, None]
        shape = (n_b, N_HEADS_C, n_s, topk, MOBA_BLOCK, HEAD_DIM)
        k_sel = cache_k[l, phys, :, hidx].reshape(shape)
        v_sel = cache_v[l, phys, :, hidx].reshape(shape)
        lg = jnp.einsum('bqhd,bhqnkd->bhqnk', qc, k_sel).astype(f32) * SCALE
        lgs.append(jnp.where(sel_ok, lg, NEG).reshape(n_b, N_HEADS_C, n_s, topk * MOBA_BLOCK))
        vals.append(v_sel.reshape(n_b, N_HEADS_C, n_s, topk * MOBA_BLOCK, HEAD_DIM))
        eqs.append('bhqk,bhqkd->bqhd')
    n_own = n_pages - n_full * ppb
    if n_own > 0:
        phys_own = page_table[:, n_full * ppb:]
        k_o = cache_k[l, phys_own, :, HC0:].reshape(n_b, n_own * page, N_HEADS_C, HEAD_DIM)
        v_o = cache_v[l, phys_own, :, HC0:].reshape(n_b, n_own * page, N_HEADS_C, HEAD_DIM)
        opos = n_full * MOBA_BLOCK + jnp.arange(n_own * page)
        ok = (opos[None, :] <= qpos[:, None]) & (opos[None, :] // MOBA_BLOCK == own[:, None])
        lg = jnp.einsum('bqhd,bkhd->bhqk', qc, k_o).astype(f32) * SCALE
        lgs.append(jnp.where(ok, lg, NEG))
        vals.append(v_o)
        eqs.append('bhqk,bkhd->bqhd')
    ok = (qpos[None, :] <= qpos[:, None]) & (own[None, :] == own[:, None])
    lg = jnp.einsum('bqhd,bkhd->bhqk', qc, kc).astype(f32) * SCALE
    lgs.append(jnp.where(ok, lg, NEG))
    vals.append(vc)
    eqs.append('bhqk,bkhd->bqhd')
    p = jax.nn.softmax(jnp.concatenate(lgs, axis=-1), axis=-1)
    ys = []
    off = 0
    for lg_i, v_i, eq in zip(lgs, vals, eqs):
        n = lg_i.shape[-1]
        ys.append(jnp.einsum(eq, p[..., off:off + n].astype(v_i.dtype), v_i))
        off += n
    y = ys[0]
    for y_i in ys[1:]:
        y = y + y_i
    return y


def _sample_mix(l, q, k, v, logf, cache_k, cache_v, cache_logf, page_table):
    f32 = jnp.float32
    n_b, n_s = q.shape[:2]
    n_pages = page_table.shape[1]
    page = cache_k.shape[2]
    p_len = n_pages * page
    qpos = p_len + jnp.arange(n_s)
    kpos = jnp.arange(p_len + n_s)

    def past(pool, h0, h1):
        return pool[l, page_table, :, h0:h1].reshape(n_b, p_len, h1 - h0, HEAD_DIM)

    qa, ka, va = q[:, :, :HB0], k[:, :, :HB0], v[:, :, :HB0]
    ka_p, va_p = past(cache_k, 0, HB0), past(cache_v, 0, HB0)
    z = jnp.concatenate([jnp.einsum('bqhd,bkhd->bhqk', qa, ka_p),
                         jnp.einsum('bqhd,bkhd->bhqk', qa, ka)], axis=-1).astype(f32) * SCALE
    w = _stick_breaking(z, kpos[None, :] < qpos[:, None]).astype(va.dtype)
    ya = (jnp.einsum('bhqk,bkhd->bqhd', w[..., :p_len], va_p)
          + jnp.einsum('bhqk,bkhd->bqhd', w[..., p_len:], va))
    qb, kb, vb = q[:, :, HB0:HC0], k[:, :, HB0:HC0], v[:, :, HB0:HC0]
    kb_p, vb_p = past(cache_k, HB0, HC0), past(cache_v, HB0, HC0)
    lf_p = cache_logf[l, page_table].reshape(n_b, p_len, N_HEADS_B).astype(f32)
    c_past = jnp.cumsum(lf_p, axis=1)
    c_new = c_past[:, -1:] + jnp.cumsum(logf, axis=1)
    c_key = jnp.concatenate([c_past, c_new], axis=1).transpose(0, 2, 1)
    lg = (jnp.concatenate([jnp.einsum('bqhd,bkhd->bhqk', qb, kb_p),
                           jnp.einsum('bqhd,bkhd->bhqk', qb, kb)], axis=-1).astype(f32) * SCALE
          + c_new.transpose(0, 2, 1)[..., None] - c_key[:, :, None, :])
    lg = jnp.where(kpos[None, :] <= qpos[:, None], lg, NEG)
    p = jax.nn.softmax(lg, axis=-1).astype(vb.dtype)
    yb = (jnp.einsum('bhqk,bkhd->bqhd', p[..., :p_len], vb_p)
          + jnp.einsum('bhqk,bkhd->bqhd', p[..., p_len:], vb))
    yc = _sample_moba(l, q[:, :, HC0:], k[:, :, HC0:], v[:, :, HC0:], qpos, cache_k, cache_v, page_table)
    return ya, yb, yc


def _swiglu(h, wg, wu, wd):
    return (jax.nn.silu(h @ wg) * (h @ wu)) @ wd


def _moe(h, router_w, router_b, wg, wu, wd):
    logits = (h @ router_w + router_b).astype(jnp.float32)
    top_v, top_i = lax.top_k(logits, TOP_K)
    wts = jax.nn.softmax(top_v, axis=-1)
    gate = jnp.sum(jax.nn.one_hot(top_i, N_EXPERTS, dtype=jnp.float32) * wts[..., None],
                   axis=-2).astype(h.dtype)
    out = gate[..., 0:1] * _swiglu(h, wg[0], wu[0], wd[0])
    for e in range(1, N_EXPERTS):
        out = out + gate[..., e:e + 1] * _swiglu(h, wg[e], wu[e], wd[e])
    return out


def _trunk(x, c, pos, mix, ada_w, ada_b, norm1_g, norm2_g, w_in, b_f, w_br_a, w_br_b, w_br_c,
           w_out, ffn_w_gate, ffn_w_up, ffn_w_down, moe_router_w, moe_router_b, moe_w_gate,
           moe_w_up, moe_w_down, final_g):
    ks, vs, lfs = [], [], []
    for l in range(DEPTH):
        mod = (jax.nn.silu(c) @ ada_w[l] + ada_b[l])[:, None, :]
        sh1, sc1, g1, sh2, sc2, g2 = jnp.split(mod, 6, axis=-1)
        h = _rmsnorm(x, norm1_g[l]) * (1.0 + sc1) + sh1
        q, k, v, logf, gates = _project(h, w_in[l], b_f[l], pos)
        ya, yb, yc = mix(l, q, k, v, logf)
        x = x + g1 * _merge(ya, yb, yc, gates, w_br_a[l], w_br_b[l], w_br_c[l], w_out[l])
        h = _rmsnorm(x, norm2_g[l]) * (1.0 + sc2) + sh2
        j = l // 2
        if l % 2 == 0:
            f = _swiglu(h, ffn_w_gate[j], ffn_w_up[j], ffn_w_down[j])
        else:
            f = _moe(h, moe_router_w[j], moe_router_b[j], moe_w_gate[j], moe_w_up[j], moe_w_down[j])
        x = x + g2 * f
        ks.append(k)
        vs.append(v)
        lfs.append(logf)
    return _rmsnorm(x, final_g), jnp.stack(ks), jnp.stack(vs), jnp.stack(lfs)


def setup_inputs(seed: int = 0) -> dict:
    key = jax.random.key(seed)
    ks = jax.random.split(key, 32)
    f32 = jnp.float32

    def nrm(k, shape, scale=1.0):
        return jax.random.normal(k, shape, f32) * scale

    n_pages = PAST_LEN // PAGE_SIZE
    n_used = DEC_BATCH * n_pages
    n_phys = n_used + (n_used + 3) // 4
    page_table = jax.random.permutation(ks[7], n_phys)[:n_used].reshape(DEC_BATCH, n_pages).astype(jnp.int32)
    hw_a = N_HEADS_A * HEAD_DIM
    hw_b = N_HEADS_B * HEAD_DIM
    hw_c = N_HEADS_C * HEAD_DIM
    return {
        'x_prompt': nrm(ks[0], (BATCH, SEQ, D_MODEL)),
        'x_sample': nrm(ks[1], (DEC_BATCH, DEC_SEQ, D_MODEL)),
        'c_prompt': nrm(ks[2], (BATCH, D_MODEL)),
        'c_sample': nrm(ks[3], (DEC_BATCH, D_MODEL)),
        'cache_k': nrm(ks[4], (DEPTH, n_phys, PAGE_SIZE, N_HEADS, HEAD_DIM)),
        'cache_v': nrm(ks[5], (DEPTH, n_phys, PAGE_SIZE, N_HEADS, HEAD_DIM)),
        'cache_logf': jax.nn.log_sigmoid(2.5 + nrm(ks[6], (DEPTH, n_phys, PAGE_SIZE, N_HEADS_B))),
        'page_table': page_table,
        'ada_w': nrm(ks[8], (DEPTH, D_MODEL, 6 * D_MODEL), 0.02),
        'ada_b': nrm(ks[9], (DEPTH, 6 * D_MODEL), 0.02),
        'norm1_g': 1.0 + nrm(ks[10], (DEPTH, D_MODEL), 0.05),
        'norm2_g': 1.0 + nrm(ks[11], (DEPTH, D_MODEL), 0.05),
        'w_in': nrm(ks[12], (DEPTH, D_MODEL, IN_COLS), D_MODEL ** -0.5),
        'b_f': jax.random.uniform(ks[13], (DEPTH, N_HEADS_B), f32, 1.0, 4.0),
        'w_br_a': nrm(ks[14], (DEPTH, hw_a, D_MODEL), hw_a ** -0.5),
        'w_br_b': nrm(ks[15], (DEPTH, hw_b, D_MODEL), hw_b ** -0.5),
        'w_br_c': nrm(ks[16], (DEPTH, hw_c, D_MODEL), hw_c ** -0.5),
        'w_out': nrm(ks[17], (DEPTH, D_MODEL, D_MODEL), D_MODEL ** -0.5),
        'ffn_w_gate': nrm(ks[18], (N_DENSE, D_MODEL, D_FF), D_MODEL ** -0.5),
        'ffn_w_up': nrm(ks[19], (N_DENSE, D_MODEL, D_FF), D_MODEL ** -0.5),
        'ffn_w_down': nrm(ks[20], (N_DENSE, D_FF, D_MODEL), D_FF ** -0.5),
        'moe_router_w': nrm(ks[21], (N_MOE, D_MODEL, N_EXPERTS), D_MODEL ** -0.5),
        'moe_router_b': nrm(ks[22], (N_MOE, N_EXPERTS), 0.01),
        'moe_w_gate': nrm(ks[23], (N_MOE, N_EXPERTS, D_MODEL, D_EXPERT), D_MODEL ** -0.5),
        'moe_w_up': nrm(ks[24], (N_MOE, N_EXPERTS, D_MODEL, D_EXPERT), D_MODEL ** -0.5),
        'moe_w_down': nrm(ks[25], (N_MOE, N_EXPERTS, D_EXPERT, D_MODEL), D_EXPERT ** -0.5),
        'final_g': 1.0 + nrm(ks[26], (D_MODEL,), 0.05),
    }


def reference(x_prompt, x_sample, c_prompt, c_sample, cache_k, cache_v, cache_logf, page_table,
              ada_w, ada_b, norm1_g, norm2_g, w_in, b_f, w_br_a, w_br_b, w_br_c, w_out,
              ffn_w_gate, ffn_w_up, ffn_w_down, moe_router_w, moe_router_b, moe_w_gate,
              moe_w_up, moe_w_down, final_g):
    weights = (ada_w, ada_b, norm1_g, norm2_g, w_in, b_f, w_br_a, w_br_b, w_br_c, w_out,
               ffn_w_gate, ffn_w_up, ffn_w_down, moe_router_w, moe_router_b, moe_w_gate,
               moe_w_up, moe_w_down, final_g)
    pos_prompt = jnp.arange(x_prompt.shape[1])
    past_len = page_table.shape[1] * cache_k.shape[2]
    pos_sample = past_len + jnp.arange(x_sample.shape[1])

    def prompt_mix(l, q, k, v, logf):
        return _prompt_mix(q, k, v, logf)

    def sample_mix(l, q, k, v, logf):
        return _sample_mix(l, q, k, v, logf, cache_k, cache_v, cache_logf, page_table)

    y_prompt, k_p, v_p, lf_p = _trunk(x_prompt, c_prompt, pos_prompt, prompt_mix, *weights)
    y_sample, k_s, v_s, lf_s = _trunk(x_sample, c_sample, pos_sample, sample_mix, *weights)
    return (y_prompt, y_sample, k_p, v_p, lf_p, k_s, v_s, lf_s)
```

```python
import functools

import jax
import jax.numpy as jnp
from jax import lax
from jax.experimental import pallas as pl
from jax.experimental.pallas import tpu as pltpu

F32 = jnp.float32
BF16 = jnp.bfloat16

D_MODEL = 1024
HEAD_DIM = 64
N_HEADS = 16
N_HEADS_A = 4
N_HEADS_B = 4
N_HEADS_C = 8
QKV_W = N_HEADS * HEAD_DIM
AB_W = (N_HEADS_A + N_HEADS_B) * HEAD_DIM
C_W = N_HEADS_C * HEAD_DIM
ROT_DIM = HEAD_DIM // 4
ROPE_THETA = 500000.0
SCALE = HEAD_DIM ** -0.5
MOBA_BLOCK = 256
MOBA_TOPK = 3
N_EXPERTS = 8
TOP_K = 2
EPS = 1e-6
NEG = -1e30

LANE = 128
SUBLANE = 8
VMEM_LIMIT_BYTES = 56 * 1024 * 1024

PAIR_W = 2 * HEAD_DIM
N_PAIRS = N_HEADS // 2
EXP_ZERO = 105.0


def _cparams(*sem):
    return pltpu.CompilerParams(dimension_semantics=sem, vmem_limit_bytes=VMEM_LIMIT_BYTES)


def _const_spec(shape):
    n = len(shape)
    return pl.BlockSpec(shape, lambda *_: (0,) * n, pipeline_mode=pl.Buffered(1))


def _dot(a, b):
    return jnp.dot(a, b, preferred_element_type=F32)


def _dot_nt(a, b):
    return lax.dot_general(a, b, (((1,), (1,)), ((), ())), preferred_element_type=F32)


def _split3(x):
    p0 = x.astype(BF16)
    r = x - p0.astype(F32)
    p1 = r.astype(BF16)
    p2 = (r - p1.astype(F32)).astype(BF16)
    return p0, p1, p2


def _dot_exact_rhs(x, m):
    p0, p1, p2 = _split3(x)
    return _dot(p0, m) + _dot(p1, m) + _dot(p2, m)


def _dot_exact_lhs(m, x):
    p0, p1, p2 = _split3(x)
    return _dot(m, p0) + _dot(m, p1) + _dot(m, p2)


def _dot_nt_f32(a, b):
    a0, a1, a2 = _split3(a)
    b0, b1, b2 = _split3(b)
    return (_dot_nt(a0, b0) + _dot_nt(a0, b1) + _dot_nt(a1, b0)
            + _dot_nt(a1, b1) + _dot_nt(a0, b2) + _dot_nt(a2, b0))


def _log_sigmoid_pair(z):
    lp = jnp.log1p(jnp.exp(-jnp.abs(z)))
    return jnp.minimum(z, 0.0) - lp, jnp.minimum(-z, 0.0) - lp


def _rmsnorm(x, g):
    r = lax.rsqrt(jnp.mean(x * x, axis=-1, keepdims=True) + EPS)
    return (x * r) * g


def _iota(shape, axis):
    return lax.broadcasted_iota(jnp.int32, shape, axis)


def _mod_kernel(c_ref, w_ref, b_ref, o_ref):
    c = c_ref[...]
    s = c * jax.nn.sigmoid(c)
    o_ref[...] = _dot(s.astype(BF16), w_ref[...].astype(BF16)) + b_ref[...]


def _modulation(c_rows, ada_w, ada_b):
    depth, _, n_out = ada_w.shape
    rows = c_rows.shape[0]
    tn = 1024
    return pl.pallas_call(
        _mod_kernel,
        grid=(depth, n_out // tn),
        in_specs=[
            pl.BlockSpec((rows, D_MODEL), lambda l, j: (0, 0)),
            pl.BlockSpec((None, D_MODEL, tn), lambda l, j: (l, 0, j)),
            pl.BlockSpec((None, 1, tn), lambda l, j: (l, 0, j)),
        ],
        out_specs=pl.BlockSpec((None, rows, tn), lambda l, j: (l, 0, j)),
        out_shape=jax.ShapeDtypeStruct((depth, rows, n_out), F32),
        compiler_params=_cparams("arbitrary", "arbitrary"),
    )(c_rows, ada_w, ada_b.reshape(depth, 1, n_out))


def _rope_slab(y, cos_t, sin_lo, sin_hi):
    return (y * cos_t + pltpu.roll(y, ROT_DIM // 2, 1) * sin_hi
            + pltpu.roll(y, LANE - ROT_DIM // 2, 1) * sin_lo)


def _inproj_kernel(x_ref, sh_ref, sc_ref, g_ref, wqkv_ref, wf_ref, bf_ref, wg_ref,
                   cos_ref, slo_ref, shi_ref,
                   qh_ref, kh_ref, vh_ref, ko_ref, vo_ref, lf_ref, gate_ref, qc_ref, *rest,
                   tm, with_kmean):
    x = x_ref[...]
    h = _rmsnorm(x, g_ref[...]) * (1.0 + sc_ref[...]) + sh_ref[...]
    hb = h.astype(BF16)
    cos_t, sin_lo, sin_hi = cos_ref[...], slo_ref[...], shi_ref[...]
    half_w = QKV_W // 2
    slabs = half_w // LANE
    for j in range(6):
        kind, half = divmod(j, 2)
        y = _dot(hb, wqkv_ref[:, j * half_w:(j + 1) * half_w])
        ys = [y[:, s * LANE:(s + 1) * LANE] for s in range(slabs)]
        if half == 1 and kind < 2:
            ys = [_rope_slab(v, cos_t, sin_lo, sin_hi) for v in ys]
        for s in range(slabs):
            col = half * half_w + s * LANE
            pair = half * slabs + s
            if kind == 0:
                qh_ref[pair] = (ys[s] * SCALE).astype(BF16)
                if half == 1:
                    qc_ref[:, s * LANE:(s + 1) * LANE] = ys[s]
            elif kind == 1:
                kh_ref[pair] = ys[s].astype(BF16)
                ko_ref[:, col:col + LANE] = ys[s]
            else:
                vh_ref[pair] = ys[s].astype(BF16)
                vo_ref[:, col:col + LANE] = ys[s]
        if with_kmean and kind == 1 and half == 1:
            km_ref = rest[0]
            nb = tm // MOBA_BLOCK
            for s in range(slabs):
                blk = ys[s].reshape(nb, MOBA_BLOCK, LANE)
                km_ref[0, :, s * LANE:(s + 1) * LANE] = jnp.sum(blk, axis=1) * (1.0 / MOBA_BLOCK)
    f = _dot_nt(wf_ref[...], hb) + bf_ref[...]
    lf_ref[...] = _log_sigmoid_pair(f)[0]
    for j in range(6):
        g = _dot(hb, wg_ref[:, j * half_w:(j + 1) * half_w])
        gate_ref[:, j * half_w:(j + 1) * half_w] = jax.nn.sigmoid(g)


def _inproj(x, sh, sc, g, wqkv, wf, bf, wg, cos_t, sin_lo, sin_hi, *, tm, with_kmean):
    t = x.shape[0]
    nt = t // tm
    per_row = sh.shape[0] != 1
    mod_spec = (pl.BlockSpec((tm, D_MODEL), lambda i: (i, 0)) if per_row
                else pl.BlockSpec((1, D_MODEL), lambda i: (0, 0)))
    row_tile = pl.BlockSpec((tm, D_MODEL), lambda i: (i, 0))
    tab_spec = pl.BlockSpec((tm, LANE), lambda i: (i, 0))
    pair_spec = pl.BlockSpec((N_PAIRS, tm, LANE), lambda i: (0, i, 0))
    out_shape = [
        jax.ShapeDtypeStruct((N_PAIRS, t, LANE), BF16),
        jax.ShapeDtypeStruct((N_PAIRS, t, LANE), BF16),
        jax.ShapeDtypeStruct((N_PAIRS, t, LANE), BF16),
        jax.ShapeDtypeStruct((t, QKV_W), F32),
        jax.ShapeDtypeStruct((t, QKV_W), F32),
        jax.ShapeDtypeStruct((SUBLANE, t), F32),
        jax.ShapeDtypeStruct((t, 3 * D_MODEL), F32),
        jax.ShapeDtypeStruct((t, C_W), F32),
    ]
    out_specs = [
        pair_spec, pair_spec, pair_spec, row_tile, row_tile,
        pl.BlockSpec((SUBLANE, tm), lambda i: (0, i)),
        pl.BlockSpec((tm, 3 * D_MODEL), lambda i: (i, 0)),
        pl.BlockSpec((tm, C_W), lambda i: (i, 0)),
    ]
    if with_kmean:
        nb = tm // MOBA_BLOCK
        out_shape.append(jax.ShapeDtypeStruct((nt, nb, C_W), F32))
        out_specs.append(pl.BlockSpec((1, nb, C_W), lambda i: (i, 0, 0)))
    return pl.pallas_call(
        functools.partial(_inproj_kernel, tm=tm, with_kmean=with_kmean),
        grid=(nt,),
        in_specs=[
            row_tile, mod_spec, mod_spec, _const_spec((1, D_MODEL)),
            _const_spec(wqkv.shape), _const_spec(wf.shape), _const_spec(bf.shape),
            _const_spec(wg.shape), tab_spec, tab_spec, tab_spec,
        ],
        out_specs=out_specs,
        out_shape=out_shape,
        compiler_params=_cparams("arbitrary"),
    )(x, sh, sc, g, wqkv, wf, bf, wg, cos_t, sin_lo, sin_hi)


def _rope_tables(pos):
    half = ROT_DIM // 2
    inv = ROPE_THETA ** (-jnp.arange(0, ROT_DIM, 2, dtype=F32) / ROT_DIM)
    ang = pos.astype(F32)[:, None] * inv[None, :]
    cos, sin = jnp.cos(ang), jnp.sin(ang)
    t = pos.shape[0]
    pad = HEAD_DIM - ROT_DIM
    ones, zeros = jnp.ones((t, pad), F32), jnp.zeros((t, pad), F32)
    zh = jnp.zeros((t, half), F32)
    cos_h = jnp.concatenate([cos, cos, ones], axis=1)
    lo_h = jnp.concatenate([-sin, zh, zeros], axis=1)
    hi_h = jnp.concatenate([zh, sin, zeros], axis=1)
    rep = LANE // HEAD_DIM
    return jnp.tile(cos_h, (1, rep)), jnp.tile(lo_h, (1, rep)), jnp.tile(hi_h, (1, rep))


def _cumsum_kernel(x_ref, o_ref):
    n = x_ref.shape[1]
    upper = (_iota((LANE, LANE), 0) <= _iota((LANE, LANE), 1)).astype(BF16)
    before = (_iota((n, n), 1) < _iota((n, n), 0)).astype(BF16)
    for h in range(x_ref.shape[0]):
        w = _dot_exact_rhs(x_ref[h], upper)
        tot = jnp.broadcast_to(w[:, LANE - 1:LANE], (n, LANE))
        o_ref[h] = w + _dot_exact_lhs(before, tot)


def _seq_cumsum(lf_rows):
    rows, t = lf_rows.shape
    n = t // LANE
    out = pl.pallas_call(
        _cumsum_kernel,
        out_shape=jax.ShapeDtypeStruct((rows, n, LANE), F32),
    )(lf_rows.reshape(rows, n, LANE))
    return out.reshape(rows, t)


def _attn_a_kernel(q_ref, k_ref, v_ref, o_ref, acc_ref, run_ref, *, tq):
    i = pl.program_id(1)
    left = _iota((tq, LANE), 1) < HEAD_DIM
    q = q_ref[0]
    zero = jnp.zeros_like(q)
    qs = (jnp.where(left, q, zero), jnp.where(left, zero, q))
    later = (_iota((tq, tq), 0) > _iota((tq, tq), 1)).astype(BF16)
    causal = _iota((tq, tq), 1) < _iota((tq, tq), 0)
    acc_ref[...] = jnp.zeros_like(acc_ref)
    run_ref[...] = jnp.zeros_like(run_ref)

    def block(j, masked):
        start = pl.multiple_of(j * tq, tq)
        kb = k_ref[0, pl.ds(start, tq), :]
        vb = v_ref[0, pl.ds(start, tq), :]
        for h in range(2):
            z = _dot_nt(qs[h], kb)
            lb, l1m = _log_sigmoid_pair(z)
            if masked:
                l1m = jnp.where(causal, l1m, 0.0)
            hi = l1m.astype(BF16)
            lo = (l1m - hi.astype(F32)).astype(BF16)
            after = _dot(hi, later) + _dot(lo, later) + run_ref[h][:, :1]
            w = jnp.exp(lb + after)
            if masked:
                w = jnp.where(causal, w, 0.0)
            acc_ref[h] += _dot(w.astype(BF16), vb)
            run_ref[h] += jnp.sum(l1m, axis=1, keepdims=True)

    block(i, True)

    def cond(c):
        return jnp.logical_and(c[0] >= 0, c[1])

    def body(c):
        block(c[0], False)
        return c[0] - 1, jnp.max(run_ref[...]) > -EXP_ZERO

    lax.while_loop(cond, body, (i - 1, jnp.max(run_ref[...]) > -EXP_ZERO))
    o_ref[...] = jnp.where(left, acc_ref[0], acc_ref[1]).astype(o_ref.dtype)


def _attn_a(qh, kh, vh, *, tq):
    t = qh.shape[1]
    pairs = N_HEADS_A // 2
    kv_spec = pl.BlockSpec((1, t, LANE), lambda p, i: (p, 0, 0))
    return pl.pallas_call(
        functools.partial(_attn_a_kernel, tq=tq),
        grid=(pairs, t // tq),
        in_specs=[pl.BlockSpec((1, tq, LANE), lambda p, i: (p, i, 0)), kv_spec, kv_spec],
        out_specs=pl.BlockSpec((tq, LANE), lambda p, i: (i, p)),
        out_shape=jax.ShapeDtypeStruct((t, pairs * LANE), BF16),
        scratch_shapes=[pltpu.VMEM((2, tq, LANE), F32), pltpu.VMEM((2, tq, LANE), F32)],
        compiler_params=_cparams("arbitrary", "arbitrary"),
    )(qh, kh, vh)


def _attn_b_kernel(q_ref, k_ref, v_ref, ck_ref, o_ref, acc_ref, m_ref, l_ref, kmax_ref,
                   *, tq, pair0):
    p = pl.program_id(0)
    i = pl.program_id(1)
    t = k_ref.shape[1]
    left = _iota((tq, LANE), 1) < HEAD_DIM
    left_k = _iota((tq, LANE), 1) < HEAD_DIM

    @pl.when(i == 0)
    def _():
        def body(j, c):
            kb = k_ref[0, pl.ds(pl.multiple_of(j * tq, tq), tq), :].astype(F32)
            sq = kb * kb
            na = jnp.max(jnp.sum(jnp.where(left_k, sq, 0.0), axis=1, keepdims=True), axis=0, keepdims=True)
            nb = jnp.max(jnp.sum(jnp.where(left_k, 0.0, sq), axis=1, keepdims=True), axis=0, keepdims=True)
            return jnp.maximum(c[0], na), jnp.maximum(c[1], nb)
        z11 = jnp.zeros((1, 1), F32)
        na, nb = lax.fori_loop(0, t // tq, body, (z11, z11))
        kmax_ref[0] = jnp.broadcast_to(jnp.sqrt(na), (SUBLANE, LANE))
        kmax_ref[1] = jnp.broadcast_to(jnp.sqrt(nb), (SUBLANE, LANE))

    q = q_ref[0]
    zero = jnp.zeros_like(q)
    qs = (jnp.where(left, q, zero), jnp.where(left, zero, q))
    qf = q.astype(F32)
    qsq = qf * qf
    qn = (jnp.sqrt(jnp.sum(jnp.where(left, qsq, 0.0), axis=1, keepdims=True)),
          jnp.sqrt(jnp.sum(jnp.where(left, 0.0, qsq), axis=1, keepdims=True)))
    causal = _iota((tq, tq), 1) <= _iota((tq, tq), 0)
    q0 = pl.multiple_of(i * tq, tq)
    rows = [(pair0 + p) * 2 + h - N_HEADS_A for h in range(2)]
    c0 = [ck_ref[pl.ds(rows[h], 1), pl.ds(q0, tq)][:, :1] for h in range(2)]

    def bias_row(h, start):
        return c0[h] - ck_ref[pl.ds(rows[h], 1), pl.ds(start, tq)]

    def block(j, first):
        start = pl.multiple_of(j * tq, tq)
        kb = k_ref[0, pl.ds(start, tq), :]
        vb = v_ref[0, pl.ds(start, tq), :]
        for h in range(2):
            s = _dot_nt(qs[h], kb) + bias_row(h, start)
            if first:
                s = jnp.where(causal, s, NEG)
                m_new = jnp.max(s, axis=1, keepdims=True)
                pw = jnp.exp(s - m_new)
                l_ref[h] = jnp.broadcast_to(jnp.sum(pw, axis=1, keepdims=True), (tq, LANE))
                acc_ref[h] = _dot(pw.astype(BF16), vb)
            else:
                m_old = m_ref[h][:, :1]
                m_new = jnp.maximum(m_old, jnp.max(s, axis=1, keepdims=True))
                alpha = jnp.exp(m_old - m_new)
                pw = jnp.exp(s - m_new)
                l_ref[h] = alpha * l_ref[h] + jnp.sum(pw, axis=1, keepdims=True)
                acc_ref[h] = alpha * acc_ref[h] + _dot(pw.astype(BF16), vb)
            m_ref[h] = jnp.broadcast_to(m_new, (tq, LANE))

    def alive(j):
        start = pl.multiple_of(jnp.maximum(j, 0) * tq, tq)
        live = None
        for h in range(2):
            bmax = jnp.max(bias_row(h, start), axis=1, keepdims=True)
            slack = qn[h] * kmax_ref[h][:1, :1] + bmax - m_ref[h][:, :1] + EXP_ZERO
            ok = jnp.max(slack) > 0.0
            live = ok if live is None else jnp.logical_or(live, ok)
        return live

    block(i, True)

    def cond(c):
        return jnp.logical_and(c[0] >= 0, c[1])

    def body(c):
        block(c[0], False)
        return c[0] - 1, alive(c[0] - 1)

    lax.while_loop(cond, body, (i - 1, alive(i - 1)))
    out = jnp.where(left, acc_ref[0] / l_ref[0], acc_ref[1] / l_ref[1])
    o_ref[...] = out.astype(o_ref.dtype)


def _attn_b(qh, kh, vh, ck, *, tq):
    t = qh.shape[1]
    pair0 = N_HEADS_A // 2
    pairs = N_HEADS_B // 2
    kv_spec = pl.BlockSpec((1, t, LANE), lambda p, i: (pair0 + p, 0, 0))
    return pl.pallas_call(
        functools.partial(_attn_b_kernel, tq=tq, pair0=pair0),
        grid=(pairs, t // tq),
        in_specs=[pl.BlockSpec((1, tq, LANE), lambda p, i: (pair0 + p, i, 0)), kv_spec, kv_spec,
                  pl.BlockSpec(ck.shape, lambda p, i: (0, 0))],
        out_specs=pl.BlockSpec((tq, LANE), lambda p, i: (i, p)),
        out_shape=jax.ShapeDtypeStruct((t, pairs * LANE), BF16),
        scratch_shapes=[pltpu.VMEM((2, tq, LANE), F32), pltpu.VMEM((2, tq, LANE), F32),
                        pltpu.VMEM((2, tq, LANE), F32), pltpu.VMEM((2, SUBLANE, LANE), F32)],
        compiler_params=_cparams("arbitrary", "arbitrary"),
    )(qh, kh, vh, ck)


def _topk_lanes(gate, blk, n_valid_rounds, n_cols):
    sel = jnp.zeros(gate.shape, jnp.bool_)
    for r in range(MOBA_TOPK):
        mx = jnp.max(gate, axis=1, keepdims=True)
        idx = jnp.min(jnp.where(gate == mx, blk, n_cols), axis=1, keepdims=True)
        hit = blk == idx
        sel = jnp.logical_or(sel, jnp.logical_and(hit, r < n_valid_rounds))
        gate = jnp.where(hit, -jnp.inf, gate)
    return sel


def _attn_c_kernel(q_ref, qf_ref, kma_ref, kmb_ref, k_ref, v_ref, o_ref, acc_ref, m_ref, l_ref,
                   *, tq, nb):
    own = pl.program_id(1)
    lane = _iota((tq, LANE), 1)
    left = lane < HEAD_DIM
    q = q_ref[0]
    zero = jnp.zeros_like(q)
    qs = (jnp.where(left, q, zero), jnp.where(left, zero, q))
    qf = qf_ref[...]
    qfs = (jnp.where(left, qf, 0.0), jnp.where(left, 0.0, qf))
    lane0 = (HEAD_DIM, 0)
    q_aug = []
    for h in range(2):
        gate = _dot_nt_f32(qfs[h], (kma_ref, kmb_ref)[h][...])
        blk = lane - lane0[h]
        valid = jnp.logical_and(blk >= 0, blk < own)
        sel = _topk_lanes(jnp.where(valid, gate, NEG), jnp.where(blk >= 0, blk, LANE), own, LANE)
        in_range = jnp.logical_and(blk >= 0, blk < nb)
        drop = jnp.where(jnp.logical_and(in_range, jnp.logical_not(sel)), -1.0, 0.0).astype(BF16)
        q_aug.append(qs[h] + drop)
    causal = _iota((tq, tq), 1) <= _iota((tq, tq), 0)

    def update(h, s, vb, first):
        if first:
            m_new = jnp.max(s, axis=1, keepdims=True)
            pw = jnp.exp(s - m_new)
            l_ref[h] = jnp.broadcast_to(jnp.sum(pw, axis=1, keepdims=True), (tq, LANE))
            acc_ref[h] = _dot(pw.astype(BF16), vb)
        else:
            m_old = m_ref[h][:, :1]
            m_new = jnp.maximum(m_old, jnp.max(s, axis=1, keepdims=True))
            alpha = jnp.exp(m_old - m_new)
            pw = jnp.exp(s - m_new)
            l_ref[h] = alpha * l_ref[h] + jnp.sum(pw, axis=1, keepdims=True)
            acc_ref[h] = alpha * acc_ref[h] + _dot(pw.astype(BF16), vb)
        m_ref[h] = jnp.broadcast_to(m_new, (tq, LANE))

    start = pl.multiple_of(own * tq, tq)
    kb = k_ref[0, pl.ds(start, tq), :]
    vb = v_ref[0, pl.ds(start, tq), :]
    for h in range(2):
        update(h, jnp.where(causal, _dot_nt(qs[h], kb), NEG), vb, True)

    lane_row = _iota((1, LANE), 1)
    keep = (jnp.where(lane_row < HEAD_DIM, 1.0, 0.0).astype(BF16),
            jnp.where(lane_row < HEAD_DIM, 0.0, 1.0).astype(BF16))

    def body(n, carry):
        st = pl.multiple_of(n * tq, tq)
        kn = k_ref[0, pl.ds(st, tq), :]
        vn = v_ref[0, pl.ds(st, tq), :]
        for h in range(2):
            flag = jnp.where(lane_row == lane0[h] + n, -NEG, 0.0).astype(BF16)
            update(h, _dot_nt(q_aug[h], kn * keep[h] + flag), vn, False)
        return carry

    lax.fori_loop(0, own, body, 0)
    out = jnp.where(left, acc_ref[0] / l_ref[0], acc_ref[1] / l_ref[1])
    o_ref[...] = out.astype(o_ref.dtype)


def _attn_c(qh, qc, kmean, kh, vh):
    t = qh.shape[1]
    tq = MOBA_BLOCK
    nb = t // tq
    assert nb <= HEAD_DIM, "gate lanes must fit beside one head"
    pair0 = (N_HEADS_A + N_HEADS_B) // 2
    pairs = N_HEADS_C // 2
    km_a = jnp.zeros((LANE, C_W), F32).at[HEAD_DIM:HEAD_DIM + nb].set(kmean)
    km_b = jnp.zeros((LANE, C_W), F32).at[:nb].set(kmean)
    kv_spec = pl.BlockSpec((1, t, LANE), lambda p, i: (pair0 + p, 0, 0))
    km_spec = pl.BlockSpec((LANE, LANE), lambda p, i: (0, p))
    return pl.pallas_call(
        functools.partial(_attn_c_kernel, tq=tq, nb=nb),
        grid=(pairs, nb),
        in_specs=[pl.BlockSpec((1, tq, LANE), lambda p, i: (pair0 + p, i, 0)),
                  pl.BlockSpec((tq, LANE), lambda p, i: (i, p)), km_spec, km_spec, kv_spec, kv_spec],
        out_specs=pl.BlockSpec((tq, LANE), lambda p, i: (i, p)),
        out_shape=jax.ShapeDtypeStruct((t, C_W), BF16),
        scratch_shapes=[pltpu.VMEM((2, tq, LANE), F32)] * 3,
        compiler_params=_cparams("arbitrary", "arbitrary"),
    )(qh, qc, km_a, km_b, kh, vh)


def _merge_kernel(x_ref, ya_ref, yb_ref, yc_ref, gate_ref, wa_ref, wb_ref, wc_ref, wo_ref,
                  g1_ref, o_ref):
    m = (gate_ref[:, :D_MODEL] * _dot(ya_ref[...], wa_ref[...])
         + gate_ref[:, D_MODEL:2 * D_MODEL] * _dot(yb_ref[...], wb_ref[...])
         + gate_ref[:, 2 * D_MODEL:] * _dot(yc_ref[...], wc_ref[...]))
    o_ref[...] = x_ref[...] + g1_ref[...] * _dot(m.astype(BF16), wo_ref[...])


def _merge(x, ya, yb, yc, gates, wa, wb, wc, wo, g1, *, tm):
    t = x.shape[0]
    per_row = g1.shape[0] != 1
    mod_spec = (pl.BlockSpec((tm, D_MODEL), lambda i: (i, 0)) if per_row
                else pl.BlockSpec((1, D_MODEL), lambda i: (0, 0)))

    def rows(w):
        return pl.BlockSpec((tm, w), lambda i: (i, 0))

    return pl.pallas_call(
        _merge_kernel,
        grid=(t // tm,),
        in_specs=[rows(D_MODEL), rows(ya.shape[1]), rows(yb.shape[1]), rows(yc.shape[1]),
                  rows(3 * D_MODEL), _const_spec(wa.shape), _const_spec(wb.shape),
                  _const_spec(wc.shape), _const_spec(wo.shape), mod_spec],
        out_specs=rows(D_MODEL),
        out_shape=jax.ShapeDtypeStruct((t, D_MODEL), F32),
        compiler_params=_cparams("arbitrary"),
    )(x, ya, yb, yc, gates, wa, wb, wc, wo, g1)


def _silu(g):
    return g * jax.nn.sigmoid(g)


def _finish(x, acc, g2, fin_ref):
    y = x + g2 * acc
    if fin_ref is not None:
        y = _rmsnorm(y, fin_ref[...])
    return y


def _ffn_kernel(x_ref, sh_ref, sc_ref, g2_ref, ng_ref, wg_ref, wu_ref, wd_ref, *rest,
                chunk, final):
    fin_ref = rest[0] if final else None
    o_ref = rest[-1]
    x = x_ref[...]
    hb = (_rmsnorm(x, ng_ref[...]) * (1.0 + sc_ref[...]) + sh_ref[...]).astype(BF16)
    acc = None
    for c in range(wg_ref.shape[1] // chunk):
        cols = slice(c * chunk, (c + 1) * chunk)
        a = (_silu(_dot(hb, wg_ref[:, cols])) * _dot(hb, wu_ref[:, cols])).astype(BF16)
        part = _dot(a, wd_ref[cols, :])
        acc = part if acc is None else acc + part
    o_ref[...] = _finish(x, acc, g2_ref[...], fin_ref)


def _ffn(x, sh, sc, g2, ng, wg, wu, wd, final_g, *, tm):
    t = x.shape[0]
    per_row = sh.shape[0] != 1
    mod_spec = (pl.BlockSpec((tm, D_MODEL), lambda i: (i, 0)) if per_row
                else pl.BlockSpec((1, D_MODEL), lambda i: (0, 0)))
    rows = pl.BlockSpec((tm, D_MODEL), lambda i: (i, 0))
    final = final_g is not None
    args = [x, sh, sc, g2, ng, wg, wu, wd] + ([final_g] if final else [])
    in_specs = [rows, mod_spec, mod_spec, mod_spec, _const_spec((1, D_MODEL)),
                _const_spec(wg.shape), _const_spec(wu.shape), _const_spec(wd.shape)]
    if final:
        in_specs.append(_const_spec((1, D_MODEL)))
    return pl.pallas_call(
        functools.partial(_ffn_kernel, chunk=256, final=final),
        grid=(t // tm,),
        in_specs=in_specs,
        out_specs=rows,
        out_shape=jax.ShapeDtypeStruct((t, D_MODEL), F32),
        compiler_params=_cparams("arbitrary"),
    )(*args)


def _moe_kernel(x_ref, sh_ref, sc_ref, g2_ref, ng_ref, rw_ref, rb_ref, wg_ref, wu_ref, wd_ref,
                *rest, tm, final):
    fin_ref = rest[0] if final else None
    o_ref, hb_ref, gate_ref, acc_ref = rest[-4:]
    e = pl.program_id(1)
    c = pl.program_id(2)
    first = jnp.logical_and(e == 0, c == 0)
    last = jnp.logical_and(e == pl.num_programs(1) - 1, c == pl.num_programs(2) - 1)

    @pl.when(first)
    def _():
        h = _rmsnorm(x_ref[...], ng_ref[...]) * (1.0 + sc_ref[...]) + sh_ref[...]
        hb_ref[...] = h.astype(BF16)
        h0, h1, h2 = _split3(h)
        w0, w1, w2 = _split3(rw_ref[...])
        logits = (_dot(h0, w0) + _dot(h0, w1) + _dot(h1, w0) + _dot(h1, w1) + _dot(h0, w2)
                  + _dot(h2, w0) + rb_ref[...])
        col = _iota((tm, LANE), 1)
        logits = jnp.where(col < N_EXPERTS, logits, -jnp.inf)
        v1 = jnp.max(logits, axis=1, keepdims=True)
        i1 = jnp.min(jnp.where(logits == v1, col, LANE), axis=1, keepdims=True)
        rest_l = jnp.where(col == i1, -jnp.inf, logits)
        v2 = jnp.max(rest_l, axis=1, keepdims=True)
        i2 = jnp.min(jnp.where(rest_l == v2, col, LANE), axis=1, keepdims=True)
        e2 = jnp.exp(v2 - v1)
        w1st = 1.0 / (1.0 + e2)
        w2nd = e2 / (1.0 + e2)
        for ex in range(N_EXPERTS):
            gate = jnp.where(i1 == ex, w1st, 0.0) + jnp.where(i2 == ex, w2nd, 0.0)
            gate_ref[ex] = jnp.broadcast_to(gate, (tm, LANE))
        acc_ref[...] = jnp.zeros_like(acc_ref)

    hb = hb_ref[...]
    a = (_silu(_dot(hb, wg_ref[0])) * _dot(hb, wu_ref[0])).astype(BF16)
    y = _dot(a, wd_ref[0])
    gate = gate_ref[e]
    for s in range(D_MODEL // LANE):
        acc_ref[:, s * LANE:(s + 1) * LANE] += gate * y[:, s * LANE:(s + 1) * LANE]

    @pl.when(last)
    def _():
        o_ref[...] = _finish(x_ref[...], acc_ref[...], g2_ref[...], fin_ref)


def _moe(x, sh, sc, g2, ng, rw, rb, wg, wu, wd, final_g, *, tm, tf):
    t = x.shape[0]
    n_exp, _, d_exp = wg.shape
    per_row = sh.shape[0] != 1
    mod_spec = (pl.BlockSpec((tm, D_MODEL), lambda i, e, c: (i, 0)) if per_row
                else pl.BlockSpec((1, D_MODEL), lambda i, e, c: (0, 0)))
    rows = pl.BlockSpec((tm, D_MODEL), lambda i, e, c: (i, 0))
    final = final_g is not None
    args = [x, sh, sc, g2, ng, rw, rb, wg, wu, wd] + ([final_g] if final else [])
    in_specs = [rows, mod_spec, mod_spec, mod_spec, _const_spec((1, D_MODEL)),
                _const_spec(rw.shape), _const_spec(rb.shape),
                pl.BlockSpec((1, D_MODEL, tf), lambda i, e, c: (e, 0, c)),
                pl.BlockSpec((1, D_MODEL, tf), lambda i, e, c: (e, 0, c)),
                pl.BlockSpec((1, tf, D_MODEL), lambda i, e, c: (e, c, 0))]
    if final:
        in_specs.append(_const_spec((1, D_MODEL)))
    return pl.pallas_call(
        functools.partial(_moe_kernel, tm=tm, final=final),
        grid=(t // tm, n_exp, d_exp // tf),
        in_specs=in_specs,
        out_specs=rows,
        out_shape=jax.ShapeDtypeStruct((t, D_MODEL), F32),
        scratch_shapes=[pltpu.VMEM((tm, D_MODEL), BF16), pltpu.VMEM((N_EXPERTS, tm, LANE), F32),
                        pltpu.VMEM((tm, D_MODEL), F32)],
        compiler_params=_cparams("arbitrary", "arbitrary", "arbitrary"),
    )(*args)


ROWS_PER_HEAD = SUBLANE
AB_ROWS = (N_HEADS_A + N_HEADS_B) * ROWS_PER_HEAD
A_ROWS = N_HEADS_A * ROWS_PER_HEAD


def _sample_ab_kernel(pt_ref, q_ref, kn_ref, vn_ref, lfn_ref, k_ref, v_ref, lf_ref,
                      o_ref, ps_ref, acc_ref, run_ref, m_ref, l_ref, carry_ref, *, n_new):
    del pt_ref
    pstep = pl.program_id(1)
    n_pages = pl.num_programs(1)
    q = q_ref[0]
    later =(_iota((LANE, LANE), 0) > _iota((LANE, LANE), 1)).astype(BF16)

    def step(kab, vab, bias8, mask_a, mask_b):
        s = _dot_nt(q, kab)
        z = s[:A_ROWS]
        lb, l1m = _log_sigmoid_pair(z)
        if mask_a is not None:
            l1m = jnp.where(mask_a, l1m, 0.0)
        hi = l1m.astype(BF16)
        lo = (l1m - hi.astype(F32)).astype(BF16)
        after = _dot(hi, later) + _dot(lo, later) + run_ref[...][:, :1]
        w = jnp.exp(lb + after)
        if mask_a is not None:
            w = jnp.where(mask_a, w, 0.0)
        run_ref[...] += jnp.sum(l1m, axis=1, keepdims=True)
        bias = jnp.concatenate([jnp.broadcast_to(bias8[h:h + 1], (ROWS_PER_HEAD, LANE))
                                for h in range(N_HEADS_B)], axis=0)
        lg = s[A_ROWS:] + bias
        if mask_b is not None:
            lg = jnp.where(mask_b, lg, NEG)
        m_old = m_ref[...][:, :1]
        m_new = jnp.maximum(m_old, jnp.max(lg, axis=1, keepdims=True))
        alpha = jnp.exp(m_old - m_new)
        pw = jnp.exp(lg - m_new)
        l_ref[...] = alpha * l_ref[...] + jnp.sum(pw, axis=1, keepdims=True)
        m_ref[...] = jnp.broadcast_to(m_new, m_ref.shape)
        scale = jnp.concatenate([jnp.ones((A_ROWS, 1), F32), alpha], axis=0)
        pv = _dot(jnp.concatenate([w, pw], axis=0).astype(BF16), vab)
        acc_ref[...] = scale * acc_ref[...] + pv

    @pl.when(pstep == 0)
    def _():
        acc_ref[...] = jnp.zeros_like(acc_ref)
        run_ref[...] = jnp.zeros_like(run_ref)
        m_ref[...] = jnp.full(m_ref.shape, NEG, F32)
        l_ref[...] = jnp.zeros_like(l_ref)
        carry_ref[...] = jnp.zeros_like(carry_ref)
        upto = (_iota((LANE, LANE), 0) <= _iota((LANE, LANE), 1)).astype(BF16)
        cum_new = _dot_exact_rhs(lfn_ref[0], upto)
        qa = _iota((A_ROWS, LANE), 0) % ROWS_PER_HEAD
        la = _iota((A_ROWS, LANE), 1)
        qb = _iota((AB_ROWS - A_ROWS, LANE), 0) % ROWS_PER_HEAD
        lb_ = _iota((AB_ROWS - A_ROWS, LANE), 1)
        mask_a = jnp.logical_and(la < qa, la < n_new)
        mask_b = jnp.logical_and(lb_ <= qb, lb_ < n_new)
        step(kn_ref[0], vn_ref[0], -cum_new, mask_a, mask_b)

    kp = k_ref[0, 0]
    lf8 = lf_ref[0, 0]
    bias8 = _dot_exact_rhs(lf8, later) + carry_ref[...][:, :1]
    step(kp[:, :AB_W].astype(BF16), v_ref[0, 0].astype(BF16), bias8, None, None)
    carry_ref[...] += jnp.sum(lf8, axis=1, keepdims=True)
    ps_ref[0, pl.ds(n_pages - 1 - pstep, 1), :] = jnp.sum(kp[:, AB_W:], axis=0, keepdims=True)

    @pl.when(pstep == n_pages - 1)
    def _():
        acc = acc_ref[...]
        l_full = jnp.concatenate([jnp.ones((A_ROWS, 1), F32), l_ref[...][:, :1]], axis=0)
        y = acc / l_full
        r2 = _iota((AB_ROWS, AB_W), 0) // ROWS_PER_HEAD
        c2 = _iota((AB_ROWS, AB_W), 1) // HEAD_DIM
        y = jnp.where(r2 == c2, y, 0.0)
        o_ref[0] = jnp.sum(y.reshape(AB_ROWS // ROWS_PER_HEAD, ROWS_PER_HEAD, AB_W), axis=0)


def _sample_ab(layer, page_table, q_bd, k_new, v_new, lf_new, cache_k, cache_v, cache_lf, *, n_new):
    n_b, n_pages = page_table.shape
    page = cache_k.shape[2]

    def page_idx(b, p, pt):
        return pt[b, n_pages - 1 - p]

    grid_spec = pltpu.PrefetchScalarGridSpec(
        num_scalar_prefetch=1,
        grid=(n_b, n_pages),
        in_specs=[
            pl.BlockSpec((1, AB_ROWS, AB_W), lambda b, p, pt: (b, 0, 0)),
            pl.BlockSpec((1, page, AB_W), lambda b, p, pt: (b, 0, 0)),
            pl.BlockSpec((1, page, AB_W), lambda b, p, pt: (b, 0, 0)),
            pl.BlockSpec((1, SUBLANE, LANE), lambda b, p, pt: (b, 0, 0)),
            pl.BlockSpec((1, 1, page, QKV_W), lambda b, p, pt: (layer, page_idx(b, p, pt), 0, 0)),
            pl.BlockSpec((1, 1, page, AB_W), lambda b, p, pt: (layer, page_idx(b, p, pt), 0, 0)),
            pl.BlockSpec((1, 1, SUBLANE, page), lambda b, p, pt: (layer, page_idx(b, p, pt), 0, 0)),
        ],
        out_specs=[
            pl.BlockSpec((1, ROWS_PER_HEAD, AB_W), lambda b, p, pt: (b, 0, 0)),
            pl.BlockSpec((1, n_pages, C_W), lambda b, p, pt: (b, 0, 0)),
        ],
        scratch_shapes=[
            pltpu.VMEM((AB_ROWS, AB_W), F32),
            pltpu.VMEM((A_ROWS, LANE), F32),
            pltpu.VMEM((AB_ROWS - A_ROWS, LANE), F32),
            pltpu.VMEM((AB_ROWS - A_ROWS, LANE), F32),
            pltpu.VMEM((SUBLANE, LANE), F32),
        ],
    )
    return pl.pallas_call(
        functools.partial(_sample_ab_kernel, n_new=n_new),
        grid_spec=grid_spec,
        out_shape=[jax.ShapeDtypeStruct((n_b, ROWS_PER_HEAD, AB_W), F32),
                   jax.ShapeDtypeStruct((n_b, n_pages, C_W), F32)],
        compiler_params=_cparams("arbitrary", "arbitrary"),
    )(page_table, q_bd, k_new, v_new, lf_new, cache_k, cache_v, cache_lf)


C_ROWS = N_HEADS_C * ROWS_PER_HEAD


def _sample_gate_kernel(ps_ref, q_ref, o_ref, *, n_blk, pages_per_blk):
    n_pages = ps_ref.shape[1]
    pair = (_iota((n_blk, n_pages), 1) // pages_per_blk == _iota((n_blk, n_pages), 0)).astype(BF16)
    kmean = _dot_exact_lhs(pair, ps_ref[0]) * (1.0 / MOBA_BLOCK)
    gate = _dot_nt_f32(q_ref[0], kmean)
    blk = _iota(gate.shape, 1)
    picks = []
    for _ in range(MOBA_TOPK):
        mx = jnp.max(gate, axis=1, keepdims=True)
        idx = jnp.min(jnp.where(gate == mx, blk, n_blk), axis=1, keepdims=True)
        picks.append(idx)
        gate = jnp.where(blk == idx, -jnp.inf, gate)
    lane = _iota((C_ROWS, LANE), 1)
    out = jnp.zeros((C_ROWS, LANE), jnp.int32)
    for r, idx in enumerate(picks):
        out = jnp.where(lane == r, idx, out)
    o_ref[0] = out


def _sample_gate(page_sums, qc_bd, *, n_blk, pages_per_blk):
    n_b, n_pages, _ = page_sums.shape
    return pl.pallas_call(
        functools.partial(_sample_gate_kernel, n_blk=n_blk, pages_per_blk=pages_per_blk),
        grid=(n_b,),
        in_specs=[pl.BlockSpec((1, n_pages, C_W), lambda b: (b, 0, 0)),
                  pl.BlockSpec((1, C_ROWS, C_W), lambda b: (b, 0, 0))],
        out_specs=pl.BlockSpec((1, C_ROWS, LANE), lambda b: (b, 0, 0)),
        out_shape=jax.ShapeDtypeStruct((n_b, C_ROWS, LANE), jnp.int32),
        compiler_params=_cparams("arbitrary"),
    )(page_sums, qc_bd)


def _sample_c_kernel(sel_ref, pt_ref, q_ref, kn_ref, vn_ref, *rest, n_new, n_q, pages_per_blk):
    del sel_ref, pt_ref
    n_fetch = 2 * n_q * MOBA_TOPK * pages_per_blk
    k_refs = rest[:n_fetch]
    v_refs = rest[n_fetch:2 * n_fetch]
    o_ref = rest[2 * n_fetch]
    q = q_ref[0, 0]
    lane = _iota((SUBLANE, LANE), 1)
    row = _iota((SUBLANE, LANE), 0)
    left = lane < HEAD_DIM
    zero = jnp.zeros_like(q)
    qs = (jnp.where(left, q, zero), jnp.where(left, zero, q))
    kn = kn_ref[0, 0]
    vn = vn_ref[0, 0]
    new_ok = jnp.logical_and(lane <= row, lane < n_new)
    outs = []
    for h in range(2):
        s = jnp.where(new_ok, _dot_nt(qs[h], kn), NEG)
        m = jnp.max(s, axis=1, keepdims=True)
        pw = jnp.exp(s - m)
        l = jnp.sum(pw, axis=1, keepdims=True)
        acc = _dot(pw.astype(BF16), vn)
        f = h * n_q * MOBA_TOPK * pages_per_blk
        for qi in range(n_q):
            for _ in range(MOBA_TOPK * pages_per_blk):
                kb = k_refs[f][0, 0].astype(BF16)
                vb = v_refs[f][0, 0].astype(BF16)
                f += 1
                s = jnp.where(row == qi, _dot_nt(qs[h], kb), NEG)
                m_new = jnp.maximum(m, jnp.max(s, axis=1, keepdims=True))
                alpha = jnp.exp(m - m_new)
                pw = jnp.exp(s - m_new)
                l = alpha * l + jnp.sum(pw, axis=1, keepdims=True)
                acc = alpha * acc + _dot(pw.astype(BF16), vb)
                m = m_new
        outs.append(acc / l)
    o_ref[0] = jnp.where(left, outs[0], outs[1])


def _sample_c(layer, sel_flat, pt_flat, q8, k_new, v_new, cache_k, cache_v, *,
              n_b, n_pages, n_new, n_q, pages_per_blk):
    page = cache_k.shape[2]
    pair0 = (N_HEADS_A + N_HEADS_B) // 2
    pairs = N_HEADS_C // 2

    def fetch_spec(h, qi, r, half):
        def index(b, p, sel, pt):
            head = 2 * p + h
            blk = sel[((b * N_HEADS_C + head) * n_q + qi) * MOBA_TOPK + r]
            return layer, pt[b * n_pages + blk * pages_per_blk + half], 0, pair0 + p
        return pl.BlockSpec((1, 1, page, LANE), index)

    fetch = [fetch_spec(h, qi, r, half) for h in range(2) for qi in range(n_q)
             for r in range(MOBA_TOPK) for half in range(pages_per_blk)]
    small = pl.BlockSpec((1, 1, SUBLANE, LANE), lambda b, p, sel, pt: (p, b, 0, 0))
    newkv = pl.BlockSpec((1, 1, page, LANE), lambda b, p, sel, pt: (p, b, 0, 0))
    grid_spec = pltpu.PrefetchScalarGridSpec(
        num_scalar_prefetch=2,
        grid=(n_b, pairs),
        in_specs=[small, newkv, newkv] + fetch + fetch,
        out_specs=pl.BlockSpec((1, SUBLANE, LANE), lambda b, p, sel, pt: (b, 0, p)),
    )
    return pl.pallas_call(
        functools.partial(_sample_c_kernel, n_new=n_new, n_q=n_q, pages_per_blk=pages_per_blk),
        grid_spec=grid_spec,
        out_shape=jax.ShapeDtypeStruct((n_b, SUBLANE, C_W), F32),
        compiler_params=_cparams("arbitrary", "arbitrary"),
    )(sel_flat, pt_flat, q8, k_new, v_new, *([cache_k] * len(fetch)), *([cache_v] * len(fetch)))


def _pad_rows(a, rows, axis):
    pad = [(0, 0)] * a.ndim
    pad[axis] = (0, rows - a.shape[axis])
    return jnp.pad(a, pad)


def _block_diag_rows(q, n_heads):
    n_b, n_s, w = q.shape
    q8 = _pad_rows(q, ROWS_PER_HEAD, 1)
    tiled = jnp.tile(q8, (1, n_heads, 1))
    keep = (jnp.arange(n_heads * ROWS_PER_HEAD)[:, None] // ROWS_PER_HEAD
            == jnp.arange(w)[None, :] // HEAD_DIM)
    return jnp.where(keep[None], tiled, jnp.zeros((), q.dtype))


def _layer_weights(l, w_in, b_f, w_br_a, w_br_b, w_br_c, w_out):
    wi = w_in[l]
    f0 = 3 * QKV_W
    wqkv = wi[:, :f0].astype(BF16)
    wf = _pad_rows(wi[:, f0:f0 + N_HEADS_B].T, SUBLANE, 0).astype(BF16)
    bf = _pad_rows(b_f[l].reshape(N_HEADS_B, 1), SUBLANE, 0)
    wg = wi[:, f0 + N_HEADS_B:].astype(BF16)
    return (wqkv, wf, bf, wg, w_br_a[l].astype(BF16), w_br_b[l].astype(BF16),
            w_br_c[l].astype(BF16), w_out[l].astype(BF16))


def kernel(x_prompt, x_sample, c_prompt, c_sample, cache_k, cache_v, cache_logf, page_table,
           ada_w, ada_b, norm1_g, norm2_g, w_in, b_f, w_br_a, w_br_b, w_br_c, w_out,
           ffn_w_gate, ffn_w_up, ffn_w_down, moe_router_w, moe_router_b, moe_w_gate,
           moe_w_up, moe_w_down, final_g):
    depth = ada_w.shape[0]
    n_bp, t_p, _ = x_prompt.shape
    assert n_bp == 1, "one prompt sequence"
    n_b, n_s, _ = x_sample.shape
    n_phys, page = cache_k.shape[1], cache_k.shape[2]
    n_pages = page_table.shape[1]
    p_len = n_pages * page
    pages_per_blk = MOBA_BLOCK // page
    n_blk_s = p_len // MOBA_BLOCK
    assert p_len % MOBA_BLOCK == 0 and n_s <= ROWS_PER_HEAD and n_blk_s >= MOBA_TOPK
    t_s = n_b * n_s

    tm_p = min(512, t_p)
    tq = MOBA_BLOCK
    tm_s = t_s

    c_rows = _pad_rows(jnp.concatenate([c_prompt, c_sample], axis=0),
                       -(-(n_bp + n_b) // SUBLANE) * SUBLANE, 0)
    mod = _modulation(c_rows, ada_w, ada_b)

    def mod_parts(l):
        parts_p = [mod[l, :1, j * D_MODEL:(j + 1) * D_MODEL] for j in range(6)]
        parts_s = [jnp.repeat(mod[l, 1:1 + n_b, j * D_MODEL:(j + 1) * D_MODEL], n_s, axis=0)
                   for j in range(6)]
        return parts_p, parts_s

    tabs_p = _rope_tables(jnp.arange(t_p))
    tabs_s = _rope_tables(jnp.tile(p_len + jnp.arange(n_s), n_b))

    cache_k4 = cache_k.reshape(depth, n_phys, page, QKV_W)
    cache_v4 = cache_v.reshape(depth, n_phys, page, QKV_W)
    cache_lf4 = _pad_rows(jnp.swapaxes(cache_logf, 2, 3), SUBLANE, 2)
    pt_flat = page_table.reshape(-1)

    xp = x_prompt.reshape(t_p, D_MODEL)
    xs = x_sample.reshape(t_s, D_MODEL)
    outs = {k: [] for k in ("kp", "vp", "lp", "ks", "vs", "ls")}
    for l in range(depth):
        wqkv, wf, bf, wg, wa, wb, wc, wo = _layer_weights(l, w_in, b_f, w_br_a, w_br_b, w_br_c, w_out)
        (sh1p, sc1p, g1p, sh2p, sc2p, g2p), (sh1s, sc1s, g1s, sh2s, sc2s, g2s) = mod_parts(l)
        n1 = norm1_g[l].reshape(1, D_MODEL)
        n2 = norm2_g[l].reshape(1, D_MODEL)
        last = l == depth - 1
        fin = final_g.reshape(1, D_MODEL) if last else None

        qh, kh, vh, k_rows, v_rows, lf_rows, gates, qc, kmean = _inproj(
            xp, sh1p, sc1p, n1, wqkv, wf, bf, wg, *tabs_p, tm=tm_p, with_kmean=True)
        ck = _seq_cumsum(lf_rows)
        ya = _attn_a(qh, kh, vh, tq=tq)
        yb = _attn_b(qh, kh, vh, ck, tq=tq)
        yc = _attn_c(qh, qc, kmean.reshape(t_p // MOBA_BLOCK, C_W), kh, vh)
        xp = _merge(xp, ya, yb, yc, gates, wa, wb, wc, wo, g1p, tm=tm_p)
        outs["kp"].append(k_rows)
        outs["vp"].append(v_rows)
        outs["lp"].append(lf_rows[:N_HEADS_B].T)

        qh_s, kh_s, vh_s, k_rows_s, v_rows_s, lf_rows_s, gates_s, qc_s = _inproj(
            xs, sh1s, sc1s, n1, wqkv, wf, bf, wg, *tabs_s, tm=tm_s, with_kmean=False)
        ab_pairs = AB_W // LANE

        def ab_cols(a):
            return jnp.moveaxis(a[:ab_pairs], 0, 1).reshape(n_b, n_s, AB_W)

        q_bd = _block_diag_rows(ab_cols(qh_s), N_HEADS_A + N_HEADS_B)
        k_new = _pad_rows(ab_cols(kh_s), page, 1)
        v_new = _pad_rows(ab_cols(vh_s), page, 1)
        lf_new = _pad_rows(_pad_rows(
            jnp.swapaxes(lf_rows_s[:N_HEADS_B].T.reshape(n_b, n_s, N_HEADS_B), 1, 2), SUBLANE, 1), LANE, 2)
        yab_s, page_sums = _sample_ab(l, page_table, q_bd, k_new, v_new, lf_new,
                                      cache_k4, cache_v4, cache_lf4, n_new=n_s)
        qc_bd = _block_diag_rows(qc_s.reshape(n_b, n_s, C_W), N_HEADS_C)
        picks = _sample_gate(page_sums, qc_bd, n_blk=n_blk_s, pages_per_blk=pages_per_blk)
        sel = picks.reshape(n_b, N_HEADS_C, ROWS_PER_HEAD, LANE)[:, :, :n_s, :MOBA_TOPK].reshape(-1)
        c_pairs = slice(ab_pairs, N_PAIRS)

        def pair_rows(a, rows):
            return _pad_rows(a[c_pairs].reshape(N_PAIRS - ab_pairs, n_b, n_s, LANE), rows, 2)

        yc_s = _sample_c(l, sel, pt_flat, pair_rows(qh_s, SUBLANE), pair_rows(kh_s, page),
                         pair_rows(vh_s, page), cache_k4, cache_v4, n_b=n_b, n_pages=n_pages,
                         n_new=n_s, n_q=n_s, pages_per_blk=pages_per_blk)
        ya_s = yab_s[:, :n_s, :N_HEADS_A * HEAD_DIM].reshape(t_s, -1).astype(BF16)
        yb_s = yab_s[:, :n_s, N_HEADS_A * HEAD_DIM:].reshape(t_s, -1).astype(BF16)
        yc_sb = yc_s[:, :n_s].reshape(t_s, C_W).astype(BF16)
        xs = _merge(xs, ya_s, yb_s, yc_sb, gates_s, wa, wb, wc, wo, g1s, tm=tm_s)
        outs["ks"].append(k_rows_s)
        outs["vs"].append(v_rows_s)
        outs["ls"].append(lf_rows_s[:N_HEADS_B].T)

        j = l // 2
        if l % 2 == 0:
            fw = (ffn_w_gate[j].astype(BF16), ffn_w_up[j].astype(BF16), ffn_w_down[j].astype(BF16))
            xp = _ffn(xp, sh2p, sc2p, g2p, n2, *fw, fin, tm=tm_p)
            xs = _ffn(xs, sh2s, sc2s, g2s, n2, *fw, fin, tm=tm_s)
        else:
            rw = jnp.pad(moe_router_w[j], ((0, 0), (0, LANE - N_EXPERTS)))
            rb = jnp.pad(moe_router_b[j], (0, LANE - N_EXPERTS)).reshape(1, LANE)
            mw = (moe_w_gate[j].astype(BF16), moe_w_up[j].astype(BF16), moe_w_down[j].astype(BF16))
            xp = _moe(xp, sh2p, sc2p, g2p, n2, rw, rb, *mw, fin, tm=tm_p, tf=512)
            xs = _moe(xs, sh2s, sc2s, g2s, n2, rw, rb, *mw, fin, tm=tm_s, tf=512)

    def heads(rows, lead):
        return jnp.stack(rows).reshape(depth, *lead, N_HEADS, HEAD_DIM)

    return (xp.reshape(n_bp, t_p, D_MODEL), xs.reshape(n_b, n_s, D_MODEL),
            heads(outs["kp"], (n_bp, t_p)), heads(outs["vp"], (n_bp, t_p)),
            jnp.stack(outs["lp"]).reshape(depth, n_bp, t_p, N_HEADS_B),
            heads(outs["ks"], (n_b, n_s)), heads(outs["vs"], (n_b, n_s)),
            jnp.stack(outs["ls"]).reshape(depth, n_b, n_s, N_HEADS_B))
```

```python
import functools

import jax
import jax.numpy as jnp
from jax import lax
from jax.experimental import pallas as pl
from jax.experimental.pallas import tpu as pltpu

F32 = jnp.float32
BF16 = jnp.bfloat16

D_MODEL = 1024
HEAD_DIM = 64
N_HEADS = 16
N_HEADS_A = 4
N_HEADS_B = 4
N_HEADS_C = 8
QKV_W = N_HEADS * HEAD_DIM
AB_W = (N_HEADS_A + N_HEADS_B) * HEAD_DIM
C_W = N_HEADS_C * HEAD_DIM
ROT_DIM = HEAD_DIM // 4
ROPE_THETA = 500000.0
SCALE = HEAD_DIM ** -0.5
MOBA_BLOCK = 256
MOBA_TOPK = 3
N_EXPERTS = 8
TOP_K = 2
EPS = 1e-6
NEG = -1e30

LANE = 128
SUBLANE = 8
VMEM_LIMIT_BYTES = 56 * 1024 * 1024

PAIR_W = 2 * HEAD_DIM
N_PAIRS = N_HEADS // 2
EXP_ZERO = 105.0


def _cparams(*sem):
    return pltpu.CompilerParams(dimension_semantics=sem, vmem_limit_bytes=VMEM_LIMIT_BYTES)


def _const_spec(shape):
    n = len(shape)
    return pl.BlockSpec(shape, lambda *_: (0,) * n, pipeline_mode=pl.Buffered(1))


def _dot(a, b):
    return jnp.dot(a, b, preferred_element_type=F32)


def _dot_nt(a, b):
    return lax.dot_general(a, b, (((1,), (1,)), ((), ())), preferred_element_type=F32)


def _split3(x):
    p0 = x.astype(BF16)
    r = x - p0.astype(F32)
    p1 = r.astype(BF16)
    p2 = (r - p1.astype(F32)).astype(BF16)
    return p0, p1, p2


def _dot_exact_rhs(x, m):
    p0, p1, p2 = _split3(x)
    return _dot(p0, m) + _dot(p1, m) + _dot(p2, m)


def _dot_exact_lhs(m, x):
    p0, p1, p2 = _split3(x)
    return _dot(m, p0) + _dot(m, p1) + _dot(m, p2)


def _dot_nt_f32(a, b):
    a0, a1, a2 = _split3(a)
    b0, b1, b2 = _split3(b)
    return (_dot_nt(a0, b0) + _dot_nt(a0, b1) + _dot_nt(a1, b0)
            + _dot_nt(a1, b1) + _dot_nt(a0, b2) + _dot_nt(a2, b0))


def _log_sigmoid_pair(z):
    lp = jnp.log1p(jnp.exp(-jnp.abs(z)))
    return jnp.minimum(z, 0.0) - lp, jnp.minimum(-z, 0.0) - lp


def _rmsnorm(x, g):
    r = lax.rsqrt(jnp.mean(x * x, axis=-1, keepdims=True) + EPS)
    return (x * r) * g


def _iota(shape, axis):
    return lax.broadcasted_iota(jnp.int32, shape, axis)


def _mod_kernel(c_ref, w_ref, b_ref, o_ref):
    c = c_ref[...]
    s = c * jax.nn.sigmoid(c)
    o_ref[...] = _dot(s.astype(BF16), w_ref[...].astype(BF16)) + b_ref[...]


def _modulation(c_rows, ada_w, ada_b):
    depth, _, n_out = ada_w.shape
    rows = c_rows.shape[0]
    tn = 1024
    return pl.pallas_call(
        _mod_kernel,
        grid=(depth, n_out // tn),
        in_specs=[
            pl.BlockSpec((rows, D_MODEL), lambda l, j: (0, 0)),
            pl.BlockSpec((None, D_MODEL, tn), lambda l, j: (l, 0, j)),
            pl.BlockSpec((None, 1, tn), lambda l, j: (l, 0, j)),
        ],
        out_specs=pl.BlockSpec((None, rows, tn), lambda l, j: (l, 0, j)),
        out_shape=jax.ShapeDtypeStruct((depth, rows, n_out), F32),
        compiler_params=_cparams("arbitrary", "arbitrary"),
    )(c_rows, ada_w, ada_b.reshape(depth, 1, n_out))


def _rope_slab(y, cos_t, sin_lo, sin_hi):
    return (y * cos_t + pltpu.roll(y, ROT_DIM // 2, 1) * sin_hi
            + pltpu.roll(y, LANE - ROT_DIM // 2, 1) * sin_lo)


def _inproj_kernel(x_ref, sh_ref, sc_ref, g_ref, wqkv_ref, wf_ref, bf_ref, wg_ref,
                   cos_ref, slo_ref, shi_ref,
                   qh_ref, kh_ref, vh_ref, ko_ref, vo_ref, lf_ref, gate_ref, qc_ref, *rest,
                   tm, with_kmean):
    x = x_ref[...]
    h = _rmsnorm(x, g_ref[...]) * (1.0 + sc_ref[...]) + sh_ref[...]
    hb = h.astype(BF16)
    cos_t, sin_lo, sin_hi = cos_ref[...], slo_ref[...], shi_ref[...]
    half_w = QKV_W // 2
    slabs = half_w // LANE
    for j in range(6):
        kind, half = divmod(j, 2)
        y = _dot(hb, wqkv_ref[:, j * half_w:(j + 1) * half_w])
        ys = [y[:, s * LANE:(s + 1) * LANE] for s in range(slabs)]
        if half == 1 and kind < 2:
            ys = [_rope_slab(v, cos_t, sin_lo, sin_hi) for v in ys]
        for s in range(slabs):
            col = half * half_w + s * LANE
            pair = half * slabs + s
            if kind == 0:
                qh_ref[pair] = (ys[s] * SCALE).astype(BF16)
                if half == 1:
                    qc_ref[:, s * LANE:(s + 1) * LANE] = ys[s]
            elif kind == 1:
                kh_ref[pair] = ys[s].astype(BF16)
                ko_ref[:, col:col + LANE] = ys[s]
            else:
                vh_ref[pair] = ys[s].astype(BF16)
                vo_ref[:, col:col + LANE] = ys[s]
        if with_kmean and kind == 1 and half == 1:
            km_ref = rest[0]
            nb = tm // MOBA_BLOCK
            for s in range(slabs):
                blk = ys[s].reshape(nb, MOBA_BLOCK, LANE)
                km_ref[0, :, s * LANE:(s + 1) * LANE] = jnp.sum(blk, axis=1) * (1.0 / MOBA_BLOCK)
    f = _dot_nt(wf_ref[...], hb) + bf_ref[...]
    lf_ref[...] = _log_sigmoid_pair(f)[0]
    for j in range(6):
        g = _dot(hb, wg_ref[:, j * half_w:(j + 1) * half_w])
        gate_ref[:, j * half_w:(j + 1) * half_w] = jax.nn.sigmoid(g)


def _inproj(x, sh, sc, g, wqkv, wf, bf, wg, cos_t, sin_lo, sin_hi, *, tm, with_kmean):
    t = x.shape[0]
    nt = t // tm
    per_row = sh.shape[0] != 1
    mod_spec = (pl.BlockSpec((tm, D_MODEL), lambda i: (i, 0)) if per_row
                else pl.BlockSpec((1, D_MODEL), lambda i: (0, 0)))
    row_tile = pl.BlockSpec((tm, D_MODEL), lambda i: (i, 0))
    tab_spec = pl.BlockSpec((tm, LANE), lambda i: (i, 0))
    pair_spec = pl.BlockSpec((N_PAIRS, tm, LANE), lambda i: (0, i, 0))
    out_shape = [
        jax.ShapeDtypeStruct((N_PAIRS, t, LANE), BF16),
        jax.ShapeDtypeStruct((N_PAIRS, t, LANE), BF16),
        jax.ShapeDtypeStruct((N_PAIRS, t, LANE), BF16),
        jax.ShapeDtypeStruct((t, QKV_W), F32),
        jax.ShapeDtypeStruct((t, QKV_W), F32),
        jax.ShapeDtypeStruct((SUBLANE, t), F32),
        jax.ShapeDtypeStruct((t, 3 * D_MODEL), F32),
        jax.ShapeDtypeStruct((t, C_W), F32),
    ]
    out_specs = [
        pair_spec, pair_spec, pair_spec, row_tile, row_tile,
        pl.BlockSpec((SUBLANE, tm), lambda i: (0, i)),
        pl.BlockSpec((tm, 3 * D_MODEL), lambda i: (i, 0)),
        pl.BlockSpec((tm, C_W), lambda i: (i, 0)),
    ]
    if with_kmean:
        nb = tm // MOBA_BLOCK
        out_shape.append(jax.ShapeDtypeStruct((nt, nb, C_W), F32))
        out_specs.append(pl.BlockSpec((1, nb, C_W), lambda i: (i, 0, 0)))
    return pl.pallas_call(
        functools.partial(_inproj_kernel, tm=tm, with_kmean=with_kmean),
        grid=(nt,),
        in_specs=[
            row_tile, mod_spec, mod_spec, _const_spec((1, D_MODEL)),
            _const_spec(wqkv.shape), _const_spec(wf.shape), _const_spec(bf.shape),
            _const_spec(wg.shape), tab_spec, tab_spec, tab_spec,
        ],
        out_specs=out_specs,
        out_shape=out_shape,
        compiler_params=_cparams("arbitrary"),
    )(x, sh, sc, g, wqkv, wf, bf, wg, cos_t, sin_lo, sin_hi)


def _rope_tables(pos):
    half = ROT_DIM // 2
    inv = ROPE_THETA ** (-jnp.arange(0, ROT_DIM, 2, dtype=F32) / ROT_DIM)
    ang = pos.astype(F32)[:, None] * inv[None, :]
    cos, sin = jnp.cos(ang), jnp.sin(ang)
    t = pos.shape[0]
    pad = HEAD_DIM - ROT_DIM
    ones, zeros = jnp.ones((t, pad), F32), jnp.zeros((t, pad), F32)
    zh = jnp.zeros((t, half), F32)
    cos_h = jnp.concatenate([cos, cos, ones], axis=1)
    lo_h = jnp.concatenate([-sin, zh, zeros], axis=1)
    hi_h = jnp.concatenate([zh, sin, zeros], axis=1)
    rep = LANE // HEAD_DIM
    return jnp.tile(cos_h, (1, rep)), jnp.tile(lo_h, (1, rep)), jnp.tile(hi_h, (1, rep))


def _cumsum_kernel(x_ref, o_ref):
    n = x_ref.shape[1]
    upper = (_iota((LANE, LANE), 0) <= _iota((LANE, LANE), 1)).astype(BF16)
    before = (_iota((n, n), 1) < _iota((n, n), 0)).astype(BF16)
    for h in range(x_ref.shape[0]):
        w = _dot_exact_rhs(x_ref[h], upper)
        tot = jnp.broadcast_to(w[:, LANE - 1:LANE], (n, LANE))
        o_ref[h] = w + _dot_exact_lhs(before, tot)


def _seq_cumsum(lf_rows):
    rows, t = lf_rows.shape
    n = t // LANE
    out = pl.pallas_call(
        _cumsum_kernel,
        out_shape=jax.ShapeDtypeStruct((rows, n, LANE), F32),
    )(lf_rows.reshape(rows, n, LANE))
    return out.reshape(rows, t)


def _attn_a_kernel(q_ref, k_ref, v_ref, o_ref, acc_ref, run_ref, *, tq):
    i = pl.program_id(1)
    left = _iota((tq, LANE), 1) < HEAD_DIM
    q = q_ref[0]
    zero = jnp.zeros_like(q)
    qs = (jnp.where(left, q, zero), jnp.where(left, zero, q))
    later = (_iota((tq, tq), 0) > _iota((tq, tq), 1)).astype(BF16)
    causal = _iota((tq, tq), 1) < _iota((tq, tq), 0)
    acc_ref[...] = jnp.zeros_like(acc_ref)
    run_ref[...] = jnp.zeros_like(run_ref)

    def block(j, masked):
        start = pl.multiple_of(j * tq, tq)
        kb = k_ref[0, pl.ds(start, tq), :]
        vb = v_ref[0, pl.ds(start, tq), :]
        for h in range(2):
            z = _dot_nt(qs[h], kb)
            lb, l1m = _log_sigmoid_pair(z)
            if masked:
                l1m = jnp.where(causal, l1m, 0.0)
            hi = l1m.astype(BF16)
            lo = (l1m - hi.astype(F32)).astype(BF16)
            after = _dot(hi, later) + _dot(lo, later) + run_ref[h][:, :1]
            w = jnp.exp(lb + after)
            if masked:
                w = jnp.where(causal, w, 0.0)
            acc_ref[h] += _dot(w.astype(BF16), vb)
            run_ref[h] += jnp.sum(l1m, axis=1, keepdims=True)

    block(i, True)

    def cond(c):
        return jnp.logical_and(c[0] >= 0, c[1])

    def body(c):
        block(c[0], False)
        return c[0] - 1, jnp.max(run_ref[...]) > -EXP_ZERO

    lax.while_loop(cond, body, (i - 1, jnp.max(run_ref[...]) > -EXP_ZERO))
    o_ref[...] = jnp.where(left, acc_ref[0], acc_ref[1]).astype(o_ref.dtype)


def _attn_a(qh, kh, vh, *, tq):
    t = qh.shape[1]
    pairs = N_HEADS_A // 2
    kv_spec = pl.BlockSpec((1, t, LANE), lambda p, i: (p, 0, 0))
    return pl.pallas_call(
        functools.partial(_attn_a_kernel, tq=tq),
        grid=(pairs, t // tq),
        in_specs=[pl.BlockSpec((1, tq, LANE), lambda p, i: (p, i, 0)), kv_spec, kv_spec],
        out_specs=pl.BlockSpec((tq, LANE), lambda p, i: (i, p)),
        out_shape=jax.ShapeDtypeStruct((t, pairs * LANE), BF16),
        scratch_shapes=[pltpu.VMEM((2, tq, LANE), F32), pltpu.VMEM((2, tq, LANE), F32)],
        compiler_params=_cparams("arbitrary", "arbitrary"),
    )(qh, kh, vh)


ROW_CHUNK = 256


def _softmax_step(acc_ref, m_ref, l_ref, h, r0, s, vb, first):
    rows = slice(r0, r0 + s.shape[0])
    if first:
        m_new = jnp.max(s, axis=1, keepdims=True)
        pw = jnp.exp(s - m_new)
        l_ref[h, rows] = jnp.broadcast_to(jnp.sum(pw, axis=1, keepdims=True), (s.shape[0], LANE))
        acc_ref[h, rows] = _dot(pw.astype(BF16), vb)
    else:
        m_old = m_ref[h, rows][:, :1]
        m_new = jnp.maximum(m_old, jnp.max(s, axis=1, keepdims=True))
        alpha = jnp.exp(m_old - m_new)
        pw = jnp.exp(s - m_new)
        l_ref[h, rows] = alpha * l_ref[h, rows] + jnp.sum(pw, axis=1, keepdims=True)
        acc_ref[h, rows] = alpha * acc_ref[h, rows] + _dot(pw.astype(BF16), vb)
    m_ref[h, rows] = jnp.broadcast_to(m_new, (s.shape[0], LANE))


SHIFT_MARGIN = 60.0


def _head_lanes(h):
    own = (_iota((1, LANE), 1) < HEAD_DIM) == (h == 0)
    return jnp.where(own, 1.0, 0.0).astype(BF16)


def _pair_key_norm_max(k_ref, kmax_ref, tq):
    left = _iota((tq, LANE), 1) < HEAD_DIM

    def body(j, c):
        kb = k_ref[0, pl.ds(pl.multiple_of(j * tq, tq), tq), :].astype(F32)
        sq = kb * kb
        na = jnp.max(jnp.sum(jnp.where(left, sq, 0.0), axis=1, keepdims=True), axis=0, keepdims=True)
        nb = jnp.max(jnp.sum(jnp.where(left, 0.0, sq), axis=1, keepdims=True), axis=0, keepdims=True)
        return jnp.maximum(c[0], na), jnp.maximum(c[1], nb)

    z11 = jnp.zeros((1, 1), F32)
    na, nb = lax.fori_loop(0, k_ref.shape[1] // tq, body, (z11, z11))
    kmax_ref[0] = jnp.broadcast_to(jnp.sqrt(na), (SUBLANE, LANE))
    kmax_ref[1] = jnp.broadcast_to(jnp.sqrt(nb), (SUBLANE, LANE))


def _pair_query_norms(q, tq):
    left = _iota((tq, LANE), 1) < HEAD_DIM
    qf = q.astype(F32)
    qsq = qf * qf
    return (jnp.sqrt(jnp.sum(jnp.where(left, qsq, 0.0), axis=1, keepdims=True)),
            jnp.sqrt(jnp.sum(jnp.where(left, 0.0, qsq), axis=1, keepdims=True)))


def _fixed_shift_begin(acc_ref, l_ref, tq):
    left = _iota((tq, LANE), 1) < HEAD_DIM
    acc_ref[0] = jnp.where(left, acc_ref[0], l_ref[0])
    acc_ref[1] = jnp.where(left, l_ref[1], acc_ref[1])


def _fixed_shift_step(acc_ref, m_ref, h, s, vb):
    keep = _head_lanes(h)
    m = jnp.tile(m_ref[h], (1, s.shape[1] // LANE))
    pw = jnp.exp(s - m).astype(BF16)
    acc_ref[h] += _dot(pw, vb * keep + (1.0 - keep))


def _fixed_shift_end(acc_ref, l_ref, tq):
    l_ref[0] = jnp.broadcast_to(acc_ref[0][:, HEAD_DIM:HEAD_DIM + 1], (tq, LANE))
    l_ref[1] = jnp.broadcast_to(acc_ref[1][:, :1], (tq, LANE))


def _causal_rows(r0, rows, cols):
    return _iota((rows, cols), 1) <= _iota((rows, cols), 0) + r0


def _attn_b_kernel(q_ref, k_ref, v_ref, ck_ref, o_ref, acc_ref, m_ref, l_ref, kmax_ref,
                   *, tq, pair0):
    p = pl.program_id(0)
    i = pl.program_id(1)
    left = _iota((tq, LANE), 1) < HEAD_DIM

    @pl.when(i == 0)
    def _():
        _pair_key_norm_max(k_ref, kmax_ref, tq)

    q = q_ref[0]
    zero = jnp.zeros_like(q)
    qs = (jnp.where(left, q, zero), jnp.where(left, zero, q))
    qn = _pair_query_norms(q, tq)
    q0 = pl.multiple_of(i * tq, tq)
    rows = [(pair0 + p) * 2 + h - N_HEADS_A for h in range(2)]
    c0 = [ck_ref[pl.ds(rows[h], 1), pl.ds(q0, tq)][:, :1] for h in range(2)]

    def bias_row(h, start):
        return c0[h] - ck_ref[pl.ds(rows[h], 1), pl.ds(start, tq)]

    def block(j, first):
        start = pl.multiple_of(j * tq, tq)
        kb = k_ref[0, pl.ds(start, tq), :]
        vb = v_ref[0, pl.ds(start, tq), :]
        for h in range(2):
            bias = bias_row(h, start)
            for r0 in range(0, tq, ROW_CHUNK):
                s = _dot_nt(qs[h][r0:r0 + ROW_CHUNK], kb) + bias
                if first:
                    s = jnp.where(_causal_rows(r0, ROW_CHUNK, tq), s, NEG)
                _softmax_step(acc_ref, m_ref, l_ref, h, r0, s, vb, first)

    def alive(j):
        start = pl.multiple_of(jnp.maximum(j, 0) * tq, tq)
        live = None
        for h in range(2):
            bmax = jnp.max(bias_row(h, start), axis=1, keepdims=True)
            slack = qn[h] * kmax_ref[h][:1, :1] + bmax - m_ref[h][:, :1] + EXP_ZERO
            ok = jnp.max(slack) > 0.0
            live = ok if live is None else jnp.logical_or(live, ok)
        return live

    block(i, True)

    def cond(c):
        return jnp.logical_and(c[0] >= 0, c[1])

    gap = [jnp.max(qn[h] * kmax_ref[h][:1, :1] - m_ref[h][:, :1]) for h in range(2)]
    fast = jnp.maximum(gap[0], gap[1]) <= SHIFT_MARGIN

    @pl.when(fast)
    def _():
        _fixed_shift_begin(acc_ref, l_ref, tq)

        def one(j, off):
            start = pl.multiple_of(jnp.maximum(j, 0) * tq, tq)
            kb = k_ref[0, pl.ds(start, tq), :]
            vb = v_ref[0, pl.ds(start, tq), :]
            for h in range(2):
                s = _dot_nt(qs[h], kb) + (bias_row(h, start) + off)
                _fixed_shift_step(acc_ref, m_ref, h, s, vb)

        def body(c):
            one(c[0], 0.0)
            one(c[0] - 1, jnp.where(c[0] >= 1, 0.0, NEG))
            return c[0] - 2, alive(c[0] - 2)

        lax.while_loop(cond, body, (i - 1, alive(i - 1)))
        _fixed_shift_end(acc_ref, l_ref, tq)

    @pl.when(jnp.logical_not(fast))
    def _():
        def body(c):
            block(c[0], False)
            return c[0] - 1, alive(c[0] - 1)

        lax.while_loop(cond, body, (i - 1, alive(i - 1)))

    out = jnp.where(left, acc_ref[0] / l_ref[0], acc_ref[1] / l_ref[1])
    o_ref[...] = out.astype(o_ref.dtype)


def _attn_b(qh, kh, vh, ck, *, tq):
    t = qh.shape[1]
    pair0 = N_HEADS_A // 2
    pairs = N_HEADS_B // 2
    kv_spec = pl.BlockSpec((1, t, LANE), lambda p, i: (pair0 + p, 0, 0))
    return pl.pallas_call(
        functools.partial(_attn_b_kernel, tq=tq, pair0=pair0),
        grid=(pairs, t // tq),
        in_specs=[pl.BlockSpec((1, tq, LANE), lambda p, i: (pair0 + p, i, 0)), kv_spec, kv_spec,
                  pl.BlockSpec(ck.shape, lambda p, i: (0, 0))],
        out_specs=pl.BlockSpec((tq, LANE), lambda p, i: (i, p)),
        out_shape=jax.ShapeDtypeStruct((t, pairs * LANE), BF16),
        scratch_shapes=[pltpu.VMEM((2, tq, LANE), F32), pltpu.VMEM((2, tq, LANE), F32),
                        pltpu.VMEM((2, tq, LANE), F32), pltpu.VMEM((2, SUBLANE, LANE), F32)],
        compiler_params=_cparams("arbitrary", "arbitrary"),
    )(qh, kh, vh, ck)


def _topk_lanes(gate, blk, n_valid_rounds, n_cols):
    sel = jnp.zeros(gate.shape, jnp.bool_)
    for r in range(MOBA_TOPK):
        mx = jnp.max(gate, axis=1, keepdims=True)
        idx = jnp.min(jnp.where(gate == mx, blk, n_cols), axis=1, keepdims=True)
        hit = blk == idx
        sel = jnp.logical_or(sel, jnp.logical_and(hit, r < n_valid_rounds))
        gate = jnp.where(hit, -jnp.inf, gate)
    return sel


def _attn_c_kernel(q_ref, qf_ref, kma_ref, kmb_ref, k_ref, v_ref, o_ref, acc_ref, m_ref, l_ref,
                   kmax_ref, *, tq, nb):
    own = pl.program_id(1)

    @pl.when(own == 0)
    def _():
        _pair_key_norm_max(k_ref, kmax_ref, tq)

    lane = _iota((tq, LANE), 1)
    left = lane < HEAD_DIM
    q = q_ref[0]
    zero = jnp.zeros_like(q)
    qs = (jnp.where(left, q, zero), jnp.where(left, zero, q))
    qf = qf_ref[...]
    qfs = (jnp.where(left, qf, 0.0), jnp.where(left, 0.0, qf))
    lane0 = (HEAD_DIM, 0)
    q_aug = []
    for h in range(2):
        gate = _dot_nt_f32(qfs[h], (kma_ref, kmb_ref)[h][...])
        blk = lane - lane0[h]
        valid = jnp.logical_and(blk >= 0, blk < own)
        sel = _topk_lanes(jnp.where(valid, gate, NEG), jnp.where(blk >= 0, blk, LANE), own, LANE)
        in_range = jnp.logical_and(blk >= 0, blk < nb)
        drop = jnp.where(jnp.logical_and(in_range, jnp.logical_not(sel)), -1.0, 0.0).astype(BF16)
        q_aug.append(qs[h] + drop)
    start = pl.multiple_of(own * tq, tq)
    kb = k_ref[0, pl.ds(start, tq), :]
    vb = v_ref[0, pl.ds(start, tq), :]
    for h in range(2):
        for r0 in range(0, tq, ROW_CHUNK):
            s = jnp.where(_causal_rows(r0, ROW_CHUNK, tq), _dot_nt(qs[h][r0:r0 + ROW_CHUNK], kb), NEG)
            _softmax_step(acc_ref, m_ref, l_ref, h, r0, s, vb, True)

    lane_row = _iota((1, LANE), 1)

    def aug_keys(n, kn, h):
        flag = jnp.where(lane_row == lane0[h] + n, -NEG, 0.0).astype(BF16)
        return kn * _head_lanes(h) + flag

    qn = _pair_query_norms(q, tq)
    gap = [jnp.max(qn[h] * kmax_ref[h][:1, :1] - m_ref[h][:, :1]) for h in range(2)]
    fast = jnp.maximum(gap[0], gap[1]) <= SHIFT_MARGIN

    @pl.when(fast)
    def _():
        _fixed_shift_begin(acc_ref, l_ref, tq)

        def one(n):
            st = pl.multiple_of(n * tq, tq)
            kn = k_ref[0, pl.ds(st, tq), :]
            vn = v_ref[0, pl.ds(st, tq), :]
            for h in range(2):
                _fixed_shift_step(acc_ref, m_ref, h, _dot_nt(q_aug[h], aug_keys(n, kn, h)), vn)

        def body(n4, carry):
            for u in range(4):
                one(4 * n4 + u)
            return carry

        lax.fori_loop(0, own // 4, body, 0)
        base = (own // 4) * 4

        @pl.when(own % 4 >= 2)
        def _():
            one(base)
            one(base + 1)

        @pl.when(own % 2 == 1)
        def _():
            one(own - 1)

        _fixed_shift_end(acc_ref, l_ref, tq)

    @pl.when(jnp.logical_not(fast))
    def _():
        def body(n, carry):
            st = pl.multiple_of(n * tq, tq)
            kn = k_ref[0, pl.ds(st, tq), :]
            vn = v_ref[0, pl.ds(st, tq), :]
            for h in range(2):
                k_aug = aug_keys(n, kn, h)
                for r0 in range(0, tq, ROW_CHUNK):
                    s = _dot_nt(q_aug[h][r0:r0 + ROW_CHUNK], k_aug)
                    _softmax_step(acc_ref, m_ref, l_ref, h, r0, s, vn, False)
            return carry

        lax.fori_loop(0, own, body, 0)

    out = jnp.where(left, acc_ref[0] / l_ref[0], acc_ref[1] / l_ref[1])
    o_ref[...] = out.astype(o_ref.dtype)


def _attn_c(qh, qc, kmean, kh, vh):
    t = qh.shape[1]
    tq = MOBA_BLOCK
    nb = t // tq
    assert nb <= HEAD_DIM, "gate lanes must fit beside one head"
    pair0 = (N_HEADS_A + N_HEADS_B) // 2
    pairs = N_HEADS_C // 2
    km_a = jnp.zeros((LANE, C_W), F32).at[HEAD_DIM:HEAD_DIM + nb].set(kmean)
    km_b = jnp.zeros((LANE, C_W), F32).at[:nb].set(kmean)
    kv_spec = pl.BlockSpec((1, t, LANE), lambda p, i: (pair0 + p, 0, 0))
    km_spec = pl.BlockSpec((LANE, LANE), lambda p, i: (0, p))
    return pl.pallas_call(
        functools.partial(_attn_c_kernel, tq=tq, nb=nb),
        grid=(pairs, nb),
        in_specs=[pl.BlockSpec((1, tq, LANE), lambda p, i: (pair0 + p, i, 0)),
                  pl.BlockSpec((tq, LANE), lambda p, i: (i, p)), km_spec, km_spec, kv_spec, kv_spec],
        out_specs=pl.BlockSpec((tq, LANE), lambda p, i: (i, p)),
        out_shape=jax.ShapeDtypeStruct((t, C_W), BF16),
        scratch_shapes=[pltpu.VMEM((2, tq, LANE), F32)] * 3 + [pltpu.VMEM((2, SUBLANE, LANE), F32)],
        compiler_params=_cparams("arbitrary", "arbitrary"),
    )(qh, qc, km_a, km_b, kh, vh)


def _merge_kernel(x_ref, ya_ref, yb_ref, yc_ref, gate_ref, wa_ref, wb_ref, wc_ref, wo_ref,
                  g1_ref, o_ref):
    m = (gate_ref[:, :D_MODEL] * _dot(ya_ref[...], wa_ref[...])
         + gate_ref[:, D_MODEL:2 * D_MODEL] * _dot(yb_ref[...], wb_ref[...])
         + gate_ref[:, 2 * D_MODEL:] * _dot(yc_ref[...], wc_ref[...]))
    o_ref[...] = x_ref[...] + g1_ref[...] * _dot(m.astype(BF16), wo_ref[...])


def _merge(x, ya, yb, yc, gates, wa, wb, wc, wo, g1, *, tm):
    t = x.shape[0]
    per_row = g1.shape[0] != 1
    mod_spec = (pl.BlockSpec((tm, D_MODEL), lambda i: (i, 0)) if per_row
                else pl.BlockSpec((1, D_MODEL), lambda i: (0, 0)))

    def rows(w):
        return pl.BlockSpec((tm, w), lambda i: (i, 0))

    return pl.pallas_call(
        _merge_kernel,
        grid=(t // tm,),
        in_specs=[rows(D_MODEL), rows(ya.shape[1]), rows(yb.shape[1]), rows(yc.shape[1]),
                  rows(3 * D_MODEL), _const_spec(wa.shape), _const_spec(wb.shape),
                  _const_spec(wc.shape), _const_spec(wo.shape), mod_spec],
        out_specs=rows(D_MODEL),
        out_shape=jax.ShapeDtypeStruct((t, D_MODEL), F32),
        compiler_params=_cparams("arbitrary"),
    )(x, ya, yb, yc, gates, wa, wb, wc, wo, g1)


def _silu(g):
    return g * jax.nn.sigmoid(g)


def _finish(x, acc, g2, fin_ref):
    y = x + g2 * acc
    if fin_ref is not None:
        y = _rmsnorm(y, fin_ref[...])
    return y


def _ffn_kernel(x_ref, sh_ref, sc_ref, g2_ref, ng_ref, wg_ref, wu_ref, wd_ref, *rest,
                chunk, final):
    fin_ref = rest[0] if final else None
    o_ref = rest[-1]
    x = x_ref[...]
    hb = (_rmsnorm(x, ng_ref[...]) * (1.0 + sc_ref[...]) + sh_ref[...]).astype(BF16)
    acc = None
    for c in range(wg_ref.shape[1] // chunk):
        cols = slice(c * chunk, (c + 1) * chunk)
        a = (_silu(_dot(hb, wg_ref[:, cols])) * _dot(hb, wu_ref[:, cols])).astype(BF16)
        part = _dot(a, wd_ref[cols, :])
        acc = part if acc is None else acc + part
    o_ref[...] = _finish(x, acc, g2_ref[...], fin_ref)


def _ffn(x, sh, sc, g2, ng, wg, wu, wd, final_g, *, tm):
    t = x.shape[0]
    per_row = sh.shape[0] != 1
    mod_spec = (pl.BlockSpec((tm, D_MODEL), lambda i: (i, 0)) if per_row
                else pl.BlockSpec((1, D_MODEL), lambda i: (0, 0)))
    rows = pl.BlockSpec((tm, D_MODEL), lambda i: (i, 0))
    final = final_g is not None
    args = [x, sh, sc, g2, ng, wg, wu, wd] + ([final_g] if final else [])
    in_specs = [rows, mod_spec, mod_spec, mod_spec, _const_spec((1, D_MODEL)),
                _const_spec(wg.shape), _const_spec(wu.shape), _const_spec(wd.shape)]
    if final:
        in_specs.append(_const_spec((1, D_MODEL)))
    return pl.pallas_call(
        functools.partial(_ffn_kernel, chunk=256, final=final),
        grid=(t // tm,),
        in_specs=in_specs,
        out_specs=rows,
        out_shape=jax.ShapeDtypeStruct((t, D_MODEL), F32),
        compiler_params=_cparams("arbitrary"),
    )(*args)


def _moe_kernel(x_ref, sh_ref, sc_ref, g2_ref, ng_ref, rw_ref, rb_ref, wg_ref, wu_ref, wd_ref,
                *rest, tm, final):
    fin_ref = rest[0] if final else None
    o_ref, hb_ref, gate_ref, acc_ref = rest[-4:]
    e = pl.program_id(1)
    c = pl.program_id(2)
    first = jnp.logical_and(e == 0, c == 0)
    last = jnp.logical_and(e == pl.num_programs(1) - 1, c == pl.num_programs(2) - 1)

    @pl.when(first)
    def _():
        h = _rmsnorm(x_ref[...], ng_ref[...]) * (1.0 + sc_ref[...]) + sh_ref[...]
        hb_ref[...] = h.astype(BF16)
        h0, h1, h2 = _split3(h)
        w0, w1, w2 = _split3(rw_ref[...])
        logits = (_dot(h0, w0) + _dot(h0, w1) + _dot(h1, w0) + _dot(h1, w1) + _dot(h0, w2)
                  + _dot(h2, w0) + rb_ref[...])
        col = _iota((tm, LANE), 1)
        logits = jnp.where(col < N_EXPERTS, logits, -jnp.inf)
        v1 = jnp.max(logits, axis=1, keepdims=True)
        i1 = jnp.min(jnp.where(logits == v1, col, LANE), axis=1, keepdims=True)
        rest_l = jnp.where(col == i1, -jnp.inf, logits)
        v2 = jnp.max(rest_l, axis=1, keepdims=True)
        i2 = jnp.min(jnp.where(rest_l == v2, col, LANE), axis=1, keepdims=True)
        e2 = jnp.exp(v2 - v1)
        w1st = 1.0 / (1.0 + e2)
        w2nd = e2 / (1.0 + e2)
        for ex in range(N_EXPERTS):
            gate = jnp.where(i1 == ex, w1st, 0.0) + jnp.where(i2 == ex, w2nd, 0.0)
            gate_ref[ex] = jnp.broadcast_to(gate, (tm, LANE))
        acc_ref[...] = jnp.zeros_like(acc_ref)

    hb = hb_ref[...]
    a = (_silu(_dot(hb, wg_ref[0])) * _dot(hb, wu_ref[0])).astype(BF16)
    y = _dot(a, wd_ref[0])
    gate = gate_ref[e]
    for s in range(D_MODEL // LANE):
        acc_ref[:, s * LANE:(s + 1) * LANE] += gate * y[:, s * LANE:(s + 1) * LANE]

    @pl.when(last)
    def _():
        o_ref[...] = _finish(x_ref[...], acc_ref[...], g2_ref[...], fin_ref)


def _moe(x, sh, sc, g2, ng, rw, rb, wg, wu, wd, final_g, *, tm, tf):
    t = x.shape[0]
    n_exp, _, d_exp = wg.shape
    per_row = sh.shape[0] != 1
    mod_spec = (pl.BlockSpec((tm, D_MODEL), lambda i, e, c: (i, 0)) if per_row
                else pl.BlockSpec((1, D_MODEL), lambda i, e, c: (0, 0)))
    rows = pl.BlockSpec((tm, D_MODEL), lambda i, e, c: (i, 0))
    final = final_g is not None
    args = [x, sh, sc, g2, ng, rw, rb, wg, wu, wd] + ([final_g] if final else [])
    in_specs = [rows, mod_spec, mod_spec, mod_spec, _const_spec((1, D_MODEL)),
                _const_spec(rw.shape), _const_spec(rb.shape),
                pl.BlockSpec((1, D_MODEL, tf), lambda i, e, c: (e, 0, c)),
                pl.BlockSpec((1, D_MODEL, tf), lambda i, e, c: (e, 0, c)),
                pl.BlockSpec((1, tf, D_MODEL), lambda i, e, c: (e, c, 0))]
    if final:
        in_specs.append(_const_spec((1, D_MODEL)))
    return pl.pallas_call(
        functools.partial(_moe_kernel, tm=tm, final=final),
        grid=(t // tm, n_exp, d_exp // tf),
        in_specs=in_specs,
        out_specs=rows,
        out_shape=jax.ShapeDtypeStruct((t, D_MODEL), F32),
        scratch_shapes=[pltpu.VMEM((tm, D_MODEL), BF16), pltpu.VMEM((N_EXPERTS, tm, LANE), F32),
                        pltpu.VMEM((tm, D_MODEL), F32)],
        compiler_params=_cparams("arbitrary", "arbitrary", "arbitrary"),
    )(*args)


ROWS_PER_HEAD = SUBLANE
AB_ROWS = (N_HEADS_A + N_HEADS_B) * ROWS_PER_HEAD
A_ROWS = N_HEADS_A * ROWS_PER_HEAD
HEADS_PER_GROUP = SUBLANE
GROUP_ROWS = HEADS_PER_GROUP * ROWS_PER_HEAD


def _page_lane_tables(page):
    rows = jnp.arange(page * HEADS_PER_GROUP)
    gather = (rows[:, None] // HEADS_PER_GROUP == jnp.arange(page)[None, :]).astype(BF16)
    match = (jnp.arange(GROUP_ROWS)[:, None] // ROWS_PER_HEAD
             == rows[None, :] % HEADS_PER_GROUP).astype(F32)
    return gather, gather.T, match


def _native_logits(q, kflat, match, gather):
    tt = _dot_nt(q, kflat) * match
    hi = tt.astype(BF16)
    lo = (tt - hi.astype(F32)).astype(BF16)
    return _dot(hi, gather) + _dot(lo, gather)


def _native_pv(p, vflat, match, spread):
    pexp = (_dot(p.astype(BF16), spread) * match).astype(BF16)
    return _dot(pexp, vflat)


def _sample_ab_kernel(pt_ref, q_ref, kn_ref, vn_ref, lfn_ref, k0_ref, kc0_ref, v0_ref, lf0_ref,
                      k1_ref, kc1_ref, v1_ref, lf1_ref, g_ref, gt_ref, mt_ref, o_ref, ps_ref,
                      acc_ref, run_ref, m_ref, l_ref, carry_ref, *, n_new):
    del pt_ref
    pstep = pl.program_id(1)
    n_steps = pl.num_programs(1)
    q = q_ref[0]
    later = (_iota((LANE, LANE), 0) > _iota((LANE, LANE), 1)).astype(BF16)

    def logits(kflat):
        return _native_logits(q, kflat, mt_ref[...], g_ref[...])

    def step(s, vflat, bias8, mask_a, mask_b):
        z = s[:A_ROWS]
        lb, l1m = _log_sigmoid_pair(z)
        if mask_a is not None:
            l1m = jnp.where(mask_a, l1m, 0.0)
        hi = l1m.astype(BF16)
        lo = (l1m - hi.astype(F32)).astype(BF16)
        after = _dot(hi, later) + _dot(lo, later) + run_ref[...][:, :1]
        w = jnp.exp(lb + after)
        if mask_a is not None:
            w = jnp.where(mask_a, w, 0.0)
        run_ref[...] += jnp.sum(l1m, axis=1, keepdims=True)
        bias = jnp.concatenate([jnp.broadcast_to(bias8[h:h + 1], (ROWS_PER_HEAD, LANE))
                                for h in range(N_HEADS_B)], axis=0)
        lg = s[A_ROWS:] + bias
        if mask_b is not None:
            lg = jnp.where(mask_b, lg, NEG)
        m_old = m_ref[...][:, :1]
        m_new = jnp.maximum(m_old, jnp.max(lg, axis=1, keepdims=True))
        alpha = jnp.exp(m_old - m_new)
        pw = jnp.exp(lg - m_new)
        l_ref[...] = alpha * l_ref[...] + jnp.sum(pw, axis=1, keepdims=True)
        m_ref[...] = jnp.broadcast_to(m_new, m_ref.shape)
        scale = jnp.concatenate([jnp.ones((A_ROWS, 1), F32), alpha], axis=0)
        pv = _native_pv(jnp.concatenate([w, pw], axis=0), vflat, mt_ref[...], gt_ref[...])
        acc_ref[...] = scale * acc_ref[...] + pv

    @pl.when(pstep == 0)
    def _():
        acc_ref[...] = jnp.zeros_like(acc_ref)
        run_ref[...] = jnp.zeros_like(run_ref)
        m_ref[...] = jnp.full(m_ref.shape, NEG, F32)
        l_ref[...] = jnp.zeros_like(l_ref)
        carry_ref[...] = jnp.zeros_like(carry_ref)
        upto = (_iota((LANE, LANE), 0) <= _iota((LANE, LANE), 1)).astype(BF16)
        cum_new = _dot_exact_rhs(lfn_ref[0], upto)
        qa = _iota((A_ROWS, LANE), 0) % ROWS_PER_HEAD
        la = _iota((A_ROWS, LANE), 1)
        qb = _iota((AB_ROWS - A_ROWS, LANE), 0) % ROWS_PER_HEAD
        lb_ = _iota((AB_ROWS - A_ROWS, LANE), 1)
        mask_a = jnp.logical_and(la < qa, la < n_new)
        mask_b = jnp.logical_and(lb_ <= qb, lb_ < n_new)
        step(logits(kn_ref[0]), vn_ref[0], -cum_new, mask_a, mask_b)

    page = k0_ref.shape[2]

    def flat(ref):
        return ref[0, 0].reshape(page * HEADS_PER_GROUP, HEAD_DIM).astype(BF16)

    s0 = logits(flat(k0_ref))
    s1 = logits(flat(k1_ref))
    for s_pg, v_ref, lf_ref in ((s0, v0_ref, lf0_ref), (s1, v1_ref, lf1_ref)):
        lf8 = lf_ref[0, 0]
        bias8 = _dot_exact_rhs(lf8, later) + carry_ref[...][:, :1]
        step(s_pg, flat(v_ref), bias8, None, None)
        carry_ref[...] += jnp.sum(lf8, axis=1, keepdims=True)
    newest = 2 * (n_steps - 1 - pstep) + 1
    ps_ref[0, newest] = jnp.sum(kc0_ref[0, 0], axis=0)
    ps_ref[0, newest - 1] = jnp.sum(kc1_ref[0, 0], axis=0)

    @pl.when(pstep == n_steps - 1)
    def _():
        l_full = jnp.concatenate([jnp.ones((A_ROWS, 1), F32), l_ref[...][:, :1]], axis=0)
        o_ref[0] = acc_ref[...] / l_full


def _sample_ab(layer, page_table, q_rows, k_new, v_new, lf_new, cache_k, cache_v, cache_lf,
               tables, *, n_new):
    n_b, n_pages = page_table.shape
    page = cache_k.shape[2]
    flat = page * HEADS_PER_GROUP

    assert n_pages % 2 == 0

    def page_idx(b, p, pt, second):
        return pt[b, n_pages - 1 - 2 * p - second]

    def group_spec(group, second):
        return pl.BlockSpec((1, 1, page, HEADS_PER_GROUP, HEAD_DIM),
                            lambda b, p, pt: (layer, page_idx(b, p, pt, second), 0, group, 0))

    def lf_spec(second):
        return pl.BlockSpec((1, 1, SUBLANE, page),
                            lambda b, p, pt: (layer, page_idx(b, p, pt, second), 0, 0))

    def const(a):
        return pl.BlockSpec(a.shape, lambda b, p, pt: (0, 0))

    grid_spec = pltpu.PrefetchScalarGridSpec(
        num_scalar_prefetch=1,
        grid=(n_b, n_pages // 2),
        in_specs=[
            pl.BlockSpec((1, GROUP_ROWS, HEAD_DIM), lambda b, p, pt: (b, 0, 0)),
            pl.BlockSpec((1, flat, HEAD_DIM), lambda b, p, pt: (b, 0, 0)),
            pl.BlockSpec((1, flat, HEAD_DIM), lambda b, p, pt: (b, 0, 0)),
            pl.BlockSpec((1, SUBLANE, LANE), lambda b, p, pt: (b, 0, 0)),
            group_spec(0, 0), group_spec(1, 0), group_spec(0, 0), lf_spec(0),
            group_spec(0, 1), group_spec(1, 1), group_spec(0, 1), lf_spec(1),
            const(tables[0]), const(tables[1]), const(tables[2]),
        ],
        out_specs=[
            pl.BlockSpec((1, GROUP_ROWS, HEAD_DIM), lambda b, p, pt: (b, 0, 0)),
            pl.BlockSpec((1, n_pages, HEADS_PER_GROUP, HEAD_DIM), lambda b, p, pt: (b, 0, 0, 0)),
        ],
        scratch_shapes=[
            pltpu.VMEM((GROUP_ROWS, HEAD_DIM), F32),
            pltpu.VMEM((A_ROWS, LANE), F32),
            pltpu.VMEM((AB_ROWS - A_ROWS, LANE), F32),
            pltpu.VMEM((AB_ROWS - A_ROWS, LANE), F32),
            pltpu.VMEM((SUBLANE, LANE), F32),
        ],
    )
    return pl.pallas_call(
        functools.partial(_sample_ab_kernel, n_new=n_new),
        grid_spec=grid_spec,
        out_shape=[jax.ShapeDtypeStruct((n_b, GROUP_ROWS, HEAD_DIM), F32),
                   jax.ShapeDtypeStruct((n_b, n_pages, HEADS_PER_GROUP, HEAD_DIM), F32)],
        compiler_params=_cparams("arbitrary", "arbitrary"),
    )(page_table, q_rows, k_new, v_new, lf_new, cache_k, cache_k, cache_v, cache_lf,
      cache_k, cache_k, cache_v, cache_lf, *tables)


C_ROWS = N_HEADS_C * ROWS_PER_HEAD


def _sample_gate_kernel(ps_ref, q_ref, o_ref, *, n_blk, pages_per_blk):
    n_pages = ps_ref.shape[1]
    pair = (_iota((n_blk, n_pages), 1) // pages_per_blk == _iota((n_blk, n_pages), 0)).astype(BF16)
    kmean = _dot_exact_lhs(pair, ps_ref[0]) * (1.0 / MOBA_BLOCK)
    gate = _dot_nt_f32(q_ref[0], kmean)
    blk = _iota(gate.shape, 1)
    picks = []
    for _ in range(MOBA_TOPK):
        mx = jnp.max(gate, axis=1, keepdims=True)
        idx = jnp.min(jnp.where(gate == mx, blk, n_blk), axis=1, keepdims=True)
        picks.append(idx)
        gate = jnp.where(blk == idx, -jnp.inf, gate)
    lane = _iota((C_ROWS, LANE), 1)
    out = jnp.zeros((C_ROWS, LANE), jnp.int32)
    for r, idx in enumerate(picks):
        out = jnp.where(lane == r, idx, out)
    o_ref[0] = out


def _sample_gate(page_sums, qc_bd, *, n_blk, pages_per_blk):
    n_b, n_pages, _ = page_sums.shape
    return pl.pallas_call(
        functools.partial(_sample_gate_kernel, n_blk=n_blk, pages_per_blk=pages_per_blk),
        grid=(n_b,),
        in_specs=[pl.BlockSpec((1, n_pages, C_W), lambda b: (b, 0, 0)),
                  pl.BlockSpec((1, C_ROWS, C_W), lambda b: (b, 0, 0))],
        out_specs=pl.BlockSpec((1, C_ROWS, LANE), lambda b: (b, 0, 0)),
        out_shape=jax.ShapeDtypeStruct((n_b, C_ROWS, LANE), jnp.int32),
        compiler_params=_cparams("arbitrary"),
    )(page_sums, qc_bd)


def _sample_c_kernel(pt_ref, bits_ref, q_ref, kn_ref, vn_ref, k0_ref, v0_ref, k1_ref, v1_ref,
                     g_ref, gt_ref, mt_ref, o_ref, acc_ref, m_ref, l_ref, *, n_new):
    del pt_ref
    b = pl.program_id(0)
    pstep = pl.program_id(1)
    n_steps = pl.num_programs(1)
    q = q_ref[0]
    row = _iota((GROUP_ROWS, LANE), 0)
    lane = _iota((GROUP_ROWS, LANE), 1)
    qi = row % ROWS_PER_HEAD

    def logits(kflat):
        return _native_logits(q, kflat, mt_ref[...], g_ref[...])

    def step(s, vflat, live):
        s = jnp.where(live, s, NEG)
        m_old = m_ref[...][:, :1]
        m_new = jnp.maximum(m_old, jnp.max(s, axis=1, keepdims=True))
        alpha = jnp.exp(m_old - m_new)
        pw = jnp.exp(s - m_new)
        l_ref[...] = alpha * l_ref[...] + jnp.sum(pw, axis=1, keepdims=True)
        m_ref[...] = jnp.broadcast_to(m_new, m_ref.shape)
        acc_ref[...] = alpha * acc_ref[...] + _native_pv(pw, vflat, mt_ref[...], gt_ref[...])

    @pl.when(pstep == 0)
    def _():
        acc_ref[...] = jnp.zeros_like(acc_ref)
        m_ref[...] = jnp.full(m_ref.shape, NEG, F32)
        l_ref[...] = jnp.zeros_like(l_ref)
        step(logits(kn_ref[0]), vn_ref[0], jnp.logical_and(lane <= qi, lane < n_new))

    word0 = bits_ref[(b * n_steps + pstep) * 2]
    word1 = bits_ref[(b * n_steps + pstep) * 2 + 1]

    @pl.when((word0 | word1) != 0)
    def _():
        page = k0_ref.shape[2]
        bit = (row // ROWS_PER_HEAD) * n_new + qi

        def flat(ref):
            return ref[0, 0].reshape(page * HEADS_PER_GROUP, HEAD_DIM).astype(BF16)

        def picked(word):
            on = (lax.shift_right_logical(jnp.broadcast_to(word, bit.shape), bit) & 1) == 1
            return jnp.logical_and(on, qi < n_new)

        s0 = logits(flat(k0_ref))
        s1 = logits(flat(k1_ref))
        step(s0, flat(v0_ref), picked(word0))
        step(s1, flat(v1_ref), picked(word1))

    @pl.when(pstep == n_steps - 1)
    def _():
        o_ref[0] = acc_ref[...] / l_ref[...][:, :1]


def _sample_c(layer, pt_flat, bits_flat, q_rows, k_new, v_new, cache_k, cache_v, tables, *,
              n_b, n_pages, n_new):
    page = cache_k.shape[2]
    flat = page * HEADS_PER_GROUP
    assert N_HEADS_C * n_new <= 32, "one bit per (head, query) in an int32"

    assert n_pages % 2 == 0

    def group_spec(second):
        return pl.BlockSpec((1, 1, page, HEADS_PER_GROUP, HEAD_DIM),
                            lambda b, p, pt, bits: (layer, pt[b * n_pages + 2 * p + second], 0, 1, 0))

    def const(a):
        return pl.BlockSpec(a.shape, lambda b, p, pt, bits: (0, 0))

    per_b = pl.BlockSpec((1, flat, HEAD_DIM), lambda b, p, pt, bits: (b, 0, 0))
    grid_spec = pltpu.PrefetchScalarGridSpec(
        num_scalar_prefetch=2,
        grid=(n_b, n_pages // 2),
        in_specs=[pl.BlockSpec((1, GROUP_ROWS, HEAD_DIM), lambda b, p, pt, bits: (b, 0, 0)),
                  per_b, per_b, group_spec(0), group_spec(0), group_spec(1), group_spec(1),
                  const(tables[0]), const(tables[1]), const(tables[2])],
        out_specs=pl.BlockSpec((1, GROUP_ROWS, HEAD_DIM), lambda b, p, pt, bits: (b, 0, 0)),
        scratch_shapes=[pltpu.VMEM((GROUP_ROWS, HEAD_DIM), F32),
                        pltpu.VMEM((GROUP_ROWS, LANE), F32), pltpu.VMEM((GROUP_ROWS, LANE), F32)],
    )
    return pl.pallas_call(
        functools.partial(_sample_c_kernel, n_new=n_new),
        grid_spec=grid_spec,
        out_shape=jax.ShapeDtypeStruct((n_b, GROUP_ROWS, HEAD_DIM), F32),
        compiler_params=_cparams("arbitrary", "arbitrary"),
    )(pt_flat, bits_flat, q_rows, k_new, v_new, cache_k, cache_v, cache_k, cache_v, *tables)


def _pad_rows(a, rows, axis):
    pad = [(0, 0)] * a.ndim
    pad[axis] = (0, rows - a.shape[axis])
    return jnp.pad(a, pad)


def _block_diag_rows(q, n_heads):
    n_b, n_s, w = q.shape
    q8 = _pad_rows(q, ROWS_PER_HEAD, 1)
    tiled = jnp.tile(q8, (1, n_heads, 1))
    keep = (jnp.arange(n_heads * ROWS_PER_HEAD)[:, None] // ROWS_PER_HEAD
            == jnp.arange(w)[None, :] // HEAD_DIM)
    return jnp.where(keep[None], tiled, jnp.zeros((), q.dtype))


def _layer_weights(l, w_in, b_f, w_br_a, w_br_b, w_br_c, w_out):
    wi = w_in[l]
    f0 = 3 * QKV_W
    wqkv = wi[:, :f0].astype(BF16)
    wf = _pad_rows(wi[:, f0:f0 + N_HEADS_B].T, SUBLANE, 0).astype(BF16)
    bf = _pad_rows(b_f[l].reshape(N_HEADS_B, 1), SUBLANE, 0)
    wg = wi[:, f0 + N_HEADS_B:].astype(BF16)
    return (wqkv, wf, bf, wg, w_br_a[l].astype(BF16), w_br_b[l].astype(BF16),
            w_br_c[l].astype(BF16), w_out[l].astype(BF16))


def kernel(x_prompt, x_sample, c_prompt, c_sample, cache_k, cache_v, cache_logf, page_table,
           ada_w, ada_b, norm1_g, norm2_g, w_in, b_f, w_br_a, w_br_b, w_br_c, w_out,
           ffn_w_gate, ffn_w_up, ffn_w_down, moe_router_w, moe_router_b, moe_w_gate,
           moe_w_up, moe_w_down, final_g):
    depth = ada_w.shape[0]
    n_bp, t_p, _ = x_prompt.shape
    assert n_bp == 1, "one prompt sequence"
    n_b, n_s, _ = x_sample.shape
    n_phys, page = cache_k.shape[1], cache_k.shape[2]
    n_pages = page_table.shape[1]
    p_len = n_pages * page
    pages_per_blk = MOBA_BLOCK // page
    n_blk_s = p_len // MOBA_BLOCK
    assert p_len % MOBA_BLOCK == 0 and n_s <= ROWS_PER_HEAD and n_blk_s >= MOBA_TOPK
    t_s = n_b * n_s

    tm_p = min(512, t_p)
    tq = MOBA_BLOCK
    tm_s = t_s

    c_rows = _pad_rows(jnp.concatenate([c_prompt, c_sample], axis=0),
                       -(-(n_bp + n_b) // SUBLANE) * SUBLANE, 0)
    mod = _modulation(c_rows, ada_w, ada_b)

    def mod_parts(l):
        parts_p = [mod[l, :1, j * D_MODEL:(j + 1) * D_MODEL] for j in range(6)]
        parts_s = [jnp.repeat(mod[l, 1:1 + n_b, j * D_MODEL:(j + 1) * D_MODEL], n_s, axis=0)
                   for j in range(6)]
        return parts_p, parts_s

    tabs_p = _rope_tables(jnp.arange(t_p))
    tabs_s = _rope_tables(jnp.tile(p_len + jnp.arange(n_s), n_b))

    cache_lf4 = _pad_rows(jnp.swapaxes(cache_logf, 2, 3), SUBLANE, 2)
    pt_flat = page_table.reshape(-1)
    tables = _page_lane_tables(page)

    xp = x_prompt.reshape(t_p, D_MODEL)
    xs = x_sample.reshape(t_s, D_MODEL)
    outs = {k: [] for k in ("kp", "vp", "lp", "ks", "vs", "ls")}
    for l in range(depth):
        wqkv, wf, bf, wg, wa, wb, wc, wo = _layer_weights(l, w_in, b_f, w_br_a, w_br_b, w_br_c, w_out)
        (sh1p, sc1p, g1p, sh2p, sc2p, g2p), (sh1s, sc1s, g1s, sh2s, sc2s, g2s) = mod_parts(l)
        n1 = norm1_g[l].reshape(1, D_MODEL)
        n2 = norm2_g[l].reshape(1, D_MODEL)
        last = l == depth - 1
        fin = final_g.reshape(1, D_MODEL) if last else None

        qh, kh, vh, k_rows, v_rows, lf_rows, gates, qc, kmean = _inproj(
            xp, sh1p, sc1p, n1, wqkv, wf, bf, wg, *tabs_p, tm=tm_p, with_kmean=True)
        ck = _seq_cumsum(lf_rows)
        ya = _attn_a(qh, kh, vh, tq=tq)
        yb = _attn_b(qh, kh, vh, ck, tq=tq)
        yc = _attn_c(qh, qc, kmean.reshape(t_p // MOBA_BLOCK, C_W), kh, vh)
        xp = _merge(xp, ya, yb, yc, gates, wa, wb, wc, wo, g1p, tm=tm_p)
        outs["kp"].append(k_rows)
        outs["vp"].append(v_rows)
        outs["lp"].append(lf_rows[:N_HEADS_B].T)

        qh_s, kh_s, vh_s, k_rows_s, v_rows_s, lf_rows_s, gates_s, qc_s = _inproj(
            xs, sh1s, sc1s, n1, wqkv, wf, bf, wg, *tabs_s, tm=tm_s, with_kmean=False)
        def group_rows(a, group):
            a4 = a.reshape(n_b, n_s, N_HEADS, HEAD_DIM)
            return a4[:, :, group * HEADS_PER_GROUP:(group + 1) * HEADS_PER_GROUP]

        def query_rows(group):
            q_cols = jnp.moveaxis(qh_s, 0, 1).reshape(t_s, QKV_W)
            q4 = _pad_rows(jnp.swapaxes(group_rows(q_cols, group), 1, 2), ROWS_PER_HEAD, 2)
            return q4.reshape(n_b, GROUP_ROWS, HEAD_DIM)

        def new_page(a, group):
            return _pad_rows(group_rows(a, group), page, 1).reshape(
                n_b, page * HEADS_PER_GROUP, HEAD_DIM).astype(BF16)

        def token_cols(y):
            y4 = y.reshape(n_b, HEADS_PER_GROUP, ROWS_PER_HEAD, HEAD_DIM)[:, :, :n_s]
            return jnp.swapaxes(y4, 1, 2).reshape(t_s, HEADS_PER_GROUP * HEAD_DIM).astype(BF16)

        lf_new = _pad_rows(_pad_rows(
            jnp.swapaxes(lf_rows_s[:N_HEADS_B].T.reshape(n_b, n_s, N_HEADS_B), 1, 2), SUBLANE, 1), LANE, 2)
        yab_s, page_sums = _sample_ab(l, page_table, query_rows(0), new_page(k_rows_s, 0),
                                      new_page(v_rows_s, 0), lf_new, cache_k, cache_v, cache_lf4,
                                      tables, n_new=n_s)
        qc_bd = _block_diag_rows(qc_s.reshape(n_b, n_s, C_W), N_HEADS_C)
        picks = _sample_gate(page_sums.reshape(n_b, n_pages, C_W), qc_bd,
                             n_blk=n_blk_s, pages_per_blk=pages_per_blk)
        blk = picks.reshape(n_b, N_HEADS_C, ROWS_PER_HEAD, LANE)[:, :, :n_s, :MOBA_TOPK]
        hit = jnp.any(blk[..., None] == jnp.arange(n_blk_s), axis=3)
        weight = jnp.left_shift(jnp.uint32(1), jnp.arange(N_HEADS_C * n_s, dtype=jnp.uint32))
        words = jnp.sum(hit.reshape(n_b, N_HEADS_C * n_s, n_blk_s) * weight[None, :, None],
                        axis=1, dtype=jnp.uint32)
        bits = lax.bitcast_convert_type(jnp.repeat(words, pages_per_blk, axis=1), jnp.int32)
        yc_s = _sample_c(l, pt_flat, bits.reshape(-1), query_rows(1), new_page(k_rows_s, 1),
                         new_page(v_rows_s, 1), cache_k, cache_v, tables,
                         n_b=n_b, n_pages=n_pages, n_new=n_s)
        yab_cols = token_cols(yab_s)
        ya_s = yab_cols[:, :N_HEADS_A * HEAD_DIM]
        yb_s = yab_cols[:, N_HEADS_A * HEAD_DIM:]
        yc_sb = token_cols(yc_s)
        xs = _merge(xs, ya_s, yb_s, yc_sb, gates_s, wa, wb, wc, wo, g1s, tm=tm_s)
        outs["ks"].append(k_rows_s)
        outs["vs"].append(v_rows_s)
        outs["ls"].append(lf_rows_s[:N_HEADS_B].T)

        j = l // 2
        if l % 2 == 0:
            fw = (ffn_w_gate[j].astype(BF16), ffn_w_up[j].astype(BF16), ffn_w_down[j].astype(BF16))
            xp = _ffn(xp, sh2p, sc2p, g2p, n2, *fw, fin, tm=tm_p)
            xs = _ffn(xs, sh2s, sc2s, g2s, n2, *fw, fin, tm=tm_s)
        else:
            rw = jnp.pad(moe_router_w[j], ((0, 0), (0, LANE - N_EXPERTS)))
            rb = jnp.pad(moe_router_b[j], (0, LANE - N_EXPERTS)).reshape(1, LANE)
            mw = (moe_w_gate[j].astype(BF16), moe_w_up[j].astype(BF16), moe_w_down[j].astype(BF16))
            xp = _moe(xp, sh2p, sc2p, g2p, n2, rw, rb, *mw, fin, tm=tm_p, tf=512)
            xs = _moe(xs, sh2s, sc2s, g2s, n2, rw, rb, *mw, fin, tm=tm_s, tf=512)

    def heads(rows, lead):
        return jnp.stack(rows).reshape(depth, *lead, N_HEADS, HEAD_DIM)

    return (xp.reshape(n_bp, t_p, D_MODEL), xs.reshape(n_b, n_s, D_MODEL),
            heads(outs["kp"], (n_bp, t_p)), heads(outs["vp"], (n_bp, t_p)),
            jnp.stack(outs["lp"]).reshape(depth, n_bp, t_p, N_HEADS_B),
            heads(outs["ks"], (n_b, n_s)), heads(outs["vs"], (n_b, n_s)),
            jnp.stack(outs["ls"]).reshape(depth, n_b, n_s, N_HEADS_B))
```

```python
import functools

import jax
import jax.numpy as jnp
from jax import lax
from jax.experimental import pallas as pl
from jax.experimental.pallas import tpu as pltpu

F32 = jnp.float32
BF16 = jnp.bfloat16

D_MODEL = 1024
HEAD_DIM = 64
N_HEADS = 16
N_HEADS_A = 4
N_HEADS_B = 4
N_HEADS_C = 8
QKV_W = N_HEADS * HEAD_DIM
AB_W = (N_HEADS_A + N_HEADS_B) * HEAD_DIM
C_W = N_HEADS_C * HEAD_DIM
ROT_DIM = HEAD_DIM // 4
ROPE_THETA = 500000.0
SCALE = HEAD_DIM ** -0.5
MOBA_BLOCK = 256
MOBA_TOPK = 3
N_EXPERTS = 8
TOP_K = 2
EPS = 1e-6
NEG = -1e30

LANE = 128
SUBLANE = 8
VMEM_LIMIT_BYTES = 56 * 1024 * 1024

PAIR_W = 2 * HEAD_DIM
N_PAIRS = N_HEADS // 2
EXP_ZERO = 105.0


def _cparams(*sem):
    return pltpu.CompilerParams(dimension_semantics=sem, vmem_limit_bytes=VMEM_LIMIT_BYTES)


def _const_spec(shape):
    n = len(shape)
    return pl.BlockSpec(shape, lambda *_: (0,) * n, pipeline_mode=pl.Buffered(1))


def _dot(a, b):
    return jnp.dot(a, b, preferred_element_type=F32)


def _dot_nt(a, b):
    return lax.dot_general(a, b, (((1,), (1,)), ((), ())), preferred_element_type=F32)


def _split3(x):
    p0 = x.astype(BF16)
    r = x - p0.astype(F32)
    p1 = r.astype(BF16)
    p2 = (r - p1.astype(F32)).astype(BF16)
    return p0, p1, p2


def _dot_exact_rhs(x, m):
    p0, p1, p2 = _split3(x)
    return _dot(p0, m) + _dot(p1, m) + _dot(p2, m)


def _dot_exact_lhs(m, x):
    p0, p1, p2 = _split3(x)
    return _dot(m, p0) + _dot(m, p1) + _dot(m, p2)


def _dot_nt_f32(a, b):
    a0, a1, a2 = _split3(a)
    b0, b1, b2 = _split3(b)
    return (_dot_nt(a0, b0) + _dot_nt(a0, b1) + _dot_nt(a1, b0)
            + _dot_nt(a1, b1) + _dot_nt(a0, b2) + _dot_nt(a2, b0))


def _log_sigmoid_pair(z):
    lp = jnp.log1p(jnp.exp(-jnp.abs(z)))
    return jnp.minimum(z, 0.0) - lp, jnp.minimum(-z, 0.0) - lp


def _rmsnorm(x, g):
    r = lax.rsqrt(jnp.mean(x * x, axis=-1, keepdims=True) + EPS)
    return (x * r) * g


def _iota(shape, axis):
    return lax.broadcasted_iota(jnp.int32, shape, axis)


def _mod_kernel(c_ref, w_ref, b_ref, o_ref):
    c = c_ref[...]
    s = c * jax.nn.sigmoid(c)
    o_ref[...] = _dot(s.astype(BF16), w_ref[...].astype(BF16)) + b_ref[...]


def _modulation(c_rows, ada_w, ada_b):
    depth, _, n_out = ada_w.shape
    rows = c_rows.shape[0]
    tn = 1024
    return pl.pallas_call(
        _mod_kernel,
        grid=(depth, n_out // tn),
        in_specs=[
            pl.BlockSpec((rows, D_MODEL), lambda l, j: (0, 0)),
            pl.BlockSpec((None, D_MODEL, tn), lambda l, j: (l, 0, j)),
            pl.BlockSpec((None, 1, tn), lambda l, j: (l, 0, j)),
        ],
        out_specs=pl.BlockSpec((None, rows, tn), lambda l, j: (l, 0, j)),
        out_shape=jax.ShapeDtypeStruct((depth, rows, n_out), F32),
        compiler_params=_cparams("arbitrary", "arbitrary"),
    )(c_rows, ada_w, ada_b.reshape(depth, 1, n_out))


def _rope_slab(y, cos_t, sin_lo, sin_hi):
    return (y * cos_t + pltpu.roll(y, ROT_DIM // 2, 1) * sin_hi
            + pltpu.roll(y, LANE - ROT_DIM // 2, 1) * sin_lo)


def _inproj_kernel(x_ref, sh_ref, sc_ref, g_ref, wqkv_ref, wf_ref, bf_ref, wg_ref,
                   cos_ref, slo_ref, shi_ref,
                   qh_ref, kh_ref, vh_ref, ko_ref, vo_ref, lf_ref, gate_ref, qc_ref, *rest,
                   tm, with_kmean):
    x = x_ref[...]
    h = _rmsnorm(x, g_ref[...]) * (1.0 + sc_ref[...]) + sh_ref[...]
    hb = h.astype(BF16)
    cos_t, sin_lo, sin_hi = cos_ref[...], slo_ref[...], shi_ref[...]
    half_w = QKV_W // 2
    slabs = half_w // LANE
    for j in range(6):
        kind, half = divmod(j, 2)
        y = _dot(hb, wqkv_ref[:, j * half_w:(j + 1) * half_w])
        ys = [y[:, s * LANE:(s + 1) * LANE] for s in range(slabs)]
        if half == 1 and kind < 2:
            ys = [_rope_slab(v, cos_t, sin_lo, sin_hi) for v in ys]
        for s in range(slabs):
            col = half * half_w + s * LANE
            pair = half * slabs + s
            if kind == 0:
                qh_ref[pair] = (ys[s] * SCALE).astype(BF16)
                if half == 1:
                    qc_ref[:, s * LANE:(s + 1) * LANE] = ys[s]
            elif kind == 1:
                kh_ref[pair] = ys[s].astype(BF16)
                ko_ref[:, col:col + LANE] = ys[s]
            else:
                vh_ref[pair] = ys[s].astype(BF16)
                vo_ref[:, col:col + LANE] = ys[s]
        if with_kmean and kind == 1 and half == 1:
            km_ref = rest[0]
            nb = tm // MOBA_BLOCK
            for s in range(slabs):
                blk = ys[s].reshape(nb, MOBA_BLOCK, LANE)
                km_ref[0, :, s * LANE:(s + 1) * LANE] = jnp.sum(blk, axis=1) * (1.0 / MOBA_BLOCK)
    f = _dot_nt(wf_ref[...], hb) + bf_ref[...]
    lf_ref[...] = _log_sigmoid_pair(f)[0]
    for j in range(6):
        g = _dot(hb, wg_ref[:, j * half_w:(j + 1) * half_w])
        gate_ref[:, j * half_w:(j + 1) * half_w] = jax.nn.sigmoid(g)


def _inproj(x, sh, sc, g, wqkv, wf, bf, wg, cos_t, sin_lo, sin_hi, *, tm, with_kmean):
    t = x.shape[0]
    nt = t // tm
    per_row = sh.shape[0] != 1
    mod_spec = (pl.BlockSpec((tm, D_MODEL), lambda i: (i, 0)) if per_row
                else pl.BlockSpec((1, D_MODEL), lambda i: (0, 0)))
    row_tile = pl.BlockSpec((tm, D_MODEL), lambda i: (i, 0))
    tab_spec = pl.BlockSpec((tm, LANE), lambda i: (i, 0))
    pair_spec = pl.BlockSpec((N_PAIRS, tm, LANE), lambda i: (0, i, 0))
    out_shape = [
        jax.ShapeDtypeStruct((N_PAIRS, t, LANE), BF16),
        jax.ShapeDtypeStruct((N_PAIRS, t, LANE), BF16),
        jax.ShapeDtypeStruct((N_PAIRS, t, LANE), BF16),
        jax.ShapeDtypeStruct((t, QKV_W), F32),
        jax.ShapeDtypeStruct((t, QKV_W), F32),
        jax.ShapeDtypeStruct((SUBLANE, t), F32),
        jax.ShapeDtypeStruct((t, 3 * D_MODEL), F32),
        jax.ShapeDtypeStruct((t, C_W), F32),
    ]
    out_specs = [
        pair_spec, pair_spec, pair_spec, row_tile, row_tile,
        pl.BlockSpec((SUBLANE, tm), lambda i: (0, i)),
        pl.BlockSpec((tm, 3 * D_MODEL), lambda i: (i, 0)),
        pl.BlockSpec((tm, C_W), lambda i: (i, 0)),
    ]
    if with_kmean:
        nb = tm // MOBA_BLOCK
        out_shape.append(jax.ShapeDtypeStruct((nt, nb, C_W), F32))
        out_specs.append(pl.BlockSpec((1, nb, C_W), lambda i: (i, 0, 0)))
    return pl.pallas_call(
        functools.partial(_inproj_kernel, tm=tm, with_kmean=with_kmean),
        grid=(nt,),
        in_specs=[
            row_tile, mod_spec, mod_spec, _const_spec((1, D_MODEL)),
            _const_spec(wqkv.shape), _const_spec(wf.shape), _const_spec(bf.shape),
            _const_spec(wg.shape), tab_spec, tab_spec, tab_spec,
        ],
        out_specs=out_specs,
        out_shape=out_shape,
        compiler_params=_cparams("arbitrary"),
    )(x, sh, sc, g, wqkv, wf, bf, wg, cos_t, sin_lo, sin_hi)


def _rope_tables(pos):
    half = ROT_DIM // 2
    inv = ROPE_THETA ** (-jnp.arange(0, ROT_DIM, 2, dtype=F32) / ROT_DIM)
    ang = pos.astype(F32)[:, None] * inv[None, :]
    cos, sin = jnp.cos(ang), jnp.sin(ang)
    t = pos.shape[0]
    pad = HEAD_DIM - ROT_DIM
    ones, zeros = jnp.ones((t, pad), F32), jnp.zeros((t, pad), F32)
    zh = jnp.zeros((t, half), F32)
    cos_h = jnp.concatenate([cos, cos, ones], axis=1)
    lo_h = jnp.concatenate([-sin, zh, zeros], axis=1)
    hi_h = jnp.concatenate([zh, sin, zeros], axis=1)
    rep = LANE // HEAD_DIM
    return jnp.tile(cos_h, (1, rep)), jnp.tile(lo_h, (1, rep)), jnp.tile(hi_h, (1, rep))


def _cumsum_kernel(x_ref, o_ref):
    n = x_ref.shape[1]
    upper = (_iota((LANE, LANE), 0) <= _iota((LANE, LANE), 1)).astype(BF16)
    before = (_iota((n, n), 1) < _iota((n, n), 0)).astype(BF16)
    for h in range(x_ref.shape[0]):
        w = _dot_exact_rhs(x_ref[h], upper)
        tot = jnp.broadcast_to(w[:, LANE - 1:LANE], (n, LANE))
        o_ref[h] = w + _dot_exact_lhs(before, tot)


def _seq_cumsum(lf_rows):
    rows, t = lf_rows.shape
    n = t // LANE
    out = pl.pallas_call(
        _cumsum_kernel,
        out_shape=jax.ShapeDtypeStruct((rows, n, LANE), F32),
    )(lf_rows.reshape(rows, n, LANE))
    return out.reshape(rows, t)


def _attn_a_kernel(q_ref, k_ref, v_ref, o_ref, acc_ref, run_ref, *, tq):
    i = pl.program_id(1)
    left = _iota((tq, LANE), 1) < HEAD_DIM
    q = q_ref[0]
    zero = jnp.zeros_like(q)
    qs = (jnp.where(left, q, zero), jnp.where(left, zero, q))
    later = (_iota((tq, tq), 0) > _iota((tq, tq), 1)).astype(BF16)
    causal = _iota((tq, tq), 1) < _iota((tq, tq), 0)
    acc_ref[...] = jnp.zeros_like(acc_ref)
    run_ref[...] = jnp.zeros_like(run_ref)

    def block(j, masked):
        start = pl.multiple_of(j * tq, tq)
        kb = k_ref[0, pl.ds(start, tq), :]
        vb = v_ref[0, pl.ds(start, tq), :]
        for h in range(2):
            z = _dot_nt(qs[h], kb)
            lb, l1m = _log_sigmoid_pair(z)
            if masked:
                l1m = jnp.where(causal, l1m, 0.0)
            hi = l1m.astype(BF16)
            lo = (l1m - hi.astype(F32)).astype(BF16)
            after = _dot(hi, later) + _dot(lo, later) + run_ref[h][:, :1]
            w = jnp.exp(lb + after)
            if masked:
                w = jnp.where(causal, w, 0.0)
            acc_ref[h] += _dot(w.astype(BF16), vb)
            run_ref[h] += jnp.sum(l1m, axis=1, keepdims=True)

    block(i, True)

    def cond(c):
        return jnp.logical_and(c[0] >= 0, c[1])

    def body(c):
        block(c[0], False)
        return c[0] - 1, jnp.max(run_ref[...]) > -EXP_ZERO

    lax.while_loop(cond, body, (i - 1, jnp.max(run_ref[...]) > -EXP_ZERO))
    o_ref[...] = jnp.where(left, acc_ref[0], acc_ref[1]).astype(o_ref.dtype)


def _attn_a(qh, kh, vh, *, tq):
    t = qh.shape[1]
    pairs = N_HEADS_A // 2
    kv_spec = pl.BlockSpec((1, t, LANE), lambda p, i: (p, 0, 0))
    return pl.pallas_call(
        functools.partial(_attn_a_kernel, tq=tq),
        grid=(pairs, t // tq),
        in_specs=[pl.BlockSpec((1, tq, LANE), lambda p, i: (p, i, 0)), kv_spec, kv_spec],
        out_specs=pl.BlockSpec((tq, LANE), lambda p, i: (i, p)),
        out_shape=jax.ShapeDtypeStruct((t, pairs * LANE), BF16),
        scratch_shapes=[pltpu.VMEM((2, tq, LANE), F32), pltpu.VMEM((2, tq, LANE), F32)],
        compiler_params=_cparams("arbitrary", "arbitrary"),
    )(qh, kh, vh)


ROW_CHUNK = 256


def _softmax_step(acc_ref, m_ref, l_ref, h, r0, s, vb, first):
    rows = slice(r0, r0 + s.shape[0])
    if first:
        m_new = jnp.max(s, axis=1, keepdims=True)
        pw = jnp.exp(s - m_new)
        l_ref[h, rows] = jnp.broadcast_to(jnp.sum(pw, axis=1, keepdims=True), (s.shape[0], LANE))
        acc_ref[h, rows] = _dot(pw.astype(BF16), vb)
    else:
        m_old = m_ref[h, rows][:, :1]
        m_new = jnp.maximum(m_old, jnp.max(s, axis=1, keepdims=True))
        alpha = jnp.exp(m_old - m_new)
        pw = jnp.exp(s - m_new)
        l_ref[h, rows] = alpha * l_ref[h, rows] + jnp.sum(pw, axis=1, keepdims=True)
        acc_ref[h, rows] = alpha * acc_ref[h, rows] + _dot(pw.astype(BF16), vb)
    m_ref[h, rows] = jnp.broadcast_to(m_new, (s.shape[0], LANE))


SHIFT_MARGIN = 60.0


def _head_lanes(h):
    own = (_iota((1, LANE), 1) < HEAD_DIM) == (h == 0)
    return jnp.where(own, 1.0, 0.0).astype(BF16)


def _pair_key_norm_max(k_ref, kmax_ref, tq):
    left = _iota((tq, LANE), 1) < HEAD_DIM

    def body(j, c):
        kb = k_ref[0, pl.ds(pl.multiple_of(j * tq, tq), tq), :].astype(F32)
        sq = kb * kb
        na = jnp.max(jnp.sum(jnp.where(left, sq, 0.0), axis=1, keepdims=True), axis=0, keepdims=True)
        nb = jnp.max(jnp.sum(jnp.where(left, 0.0, sq), axis=1, keepdims=True), axis=0, keepdims=True)
        return jnp.maximum(c[0], na), jnp.maximum(c[1], nb)

    z11 = jnp.zeros((1, 1), F32)
    na, nb = lax.fori_loop(0, k_ref.shape[1] // tq, body, (z11, z11))
    kmax_ref[0] = jnp.broadcast_to(jnp.sqrt(na), (SUBLANE, LANE))
    kmax_ref[1] = jnp.broadcast_to(jnp.sqrt(nb), (SUBLANE, LANE))


def _pair_query_norms(q, tq):
    left = _iota((tq, LANE), 1) < HEAD_DIM
    qf = q.astype(F32)
    qsq = qf * qf
    return (jnp.sqrt(jnp.sum(jnp.where(left, qsq, 0.0), axis=1, keepdims=True)),
            jnp.sqrt(jnp.sum(jnp.where(left, 0.0, qsq), axis=1, keepdims=True)))


def _fixed_shift_begin(acc_ref, l_ref, tq):
    left = _iota((tq, LANE), 1) < HEAD_DIM
    acc_ref[0] = jnp.where(left, acc_ref[0], l_ref[0])
    acc_ref[1] = jnp.where(left, l_ref[1], acc_ref[1])


def _fixed_shift_step(acc_ref, m_ref, h, s, vb):
    keep = _head_lanes(h)
    m = jnp.tile(m_ref[h], (1, s.shape[1] // LANE))
    pw = jnp.exp(s - m).astype(BF16)
    acc_ref[h] += _dot(pw, vb * keep + (1.0 - keep))


def _fixed_shift_end(acc_ref, l_ref, tq):
    l_ref[0] = jnp.broadcast_to(acc_ref[0][:, HEAD_DIM:HEAD_DIM + 1], (tq, LANE))
    l_ref[1] = jnp.broadcast_to(acc_ref[1][:, :1], (tq, LANE))


def _causal_rows(r0, rows, cols):
    return _iota((rows, cols), 1) <= _iota((rows, cols), 0) + r0


def _attn_b_kernel(q_ref, k_ref, v_ref, ck_ref, o_ref, acc_ref, m_ref, l_ref, kmax_ref,
                   *, tq, pair0):
    p = pl.program_id(0)
    i = pl.program_id(1)
    left = _iota((tq, LANE), 1) < HEAD_DIM

    @pl.when(i == 0)
    def _():
        _pair_key_norm_max(k_ref, kmax_ref, tq)

    q = q_ref[0]
    zero = jnp.zeros_like(q)
    qs = (jnp.where(left, q, zero), jnp.where(left, zero, q))
    qn = _pair_query_norms(q, tq)
    q0 = pl.multiple_of(i * tq, tq)
    rows = [(pair0 + p) * 2 + h - N_HEADS_A for h in range(2)]
    c0 = [ck_ref[pl.ds(rows[h], 1), pl.ds(q0, tq)][:, :1] for h in range(2)]

    def bias_row(h, start):
        return c0[h] - ck_ref[pl.ds(rows[h], 1), pl.ds(start, tq)]

    def block(j, first):
        start = pl.multiple_of(j * tq, tq)
        kb = k_ref[0, pl.ds(start, tq), :]
        vb = v_ref[0, pl.ds(start, tq), :]
        for h in range(2):
            bias = bias_row(h, start)
            for r0 in range(0, tq, ROW_CHUNK):
                s = _dot_nt(qs[h][r0:r0 + ROW_CHUNK], kb) + bias
                if first:
                    s = jnp.where(_causal_rows(r0, ROW_CHUNK, tq), s, NEG)
                _softmax_step(acc_ref, m_ref, l_ref, h, r0, s, vb, first)

    def alive(j):
        start = pl.multiple_of(jnp.maximum(j, 0) * tq, tq)
        live = None
        for h in range(2):
            bmax = jnp.max(bias_row(h, start), axis=1, keepdims=True)
            slack = qn[h] * kmax_ref[h][:1, :1] + bmax - m_ref[h][:, :1] + EXP_ZERO
            ok = jnp.max(slack) > 0.0
            live = ok if live is None else jnp.logical_or(live, ok)
        return live

    block(i, True)

    def cond(c):
        return jnp.logical_and(c[0] >= 0, c[1])

    gap = [jnp.max(qn[h] * kmax_ref[h][:1, :1] - m_ref[h][:, :1]) for h in range(2)]
    fast = jnp.maximum(gap[0], gap[1]) <= SHIFT_MARGIN

    @pl.when(fast)
    def _():
        _fixed_shift_begin(acc_ref, l_ref, tq)

        def one(j, off):
            start = pl.multiple_of(jnp.maximum(j, 0) * tq, tq)
            kb = k_ref[0, pl.ds(start, tq), :]
            vb = v_ref[0, pl.ds(start, tq), :]
            for h in range(2):
                s = _dot_nt(qs[h], kb) + (bias_row(h, start) + off)
                _fixed_shift_step(acc_ref, m_ref, h, s, vb)

        def body(c):
            one(c[0], 0.0)
            one(c[0] - 1, jnp.where(c[0] >= 1, 0.0, NEG))
            return c[0] - 2, alive(c[0] - 2)

        lax.while_loop(cond, body, (i - 1, alive(i - 1)))
        _fixed_shift_end(acc_ref, l_ref, tq)

    @pl.when(jnp.logical_not(fast))
    def _():
        def body(c):
            block(c[0], False)
            return c[0] - 1, alive(c[0] - 1)

        lax.while_loop(cond, body, (i - 1, alive(i - 1)))

    out = jnp.where(left, acc_ref[0] / l_ref[0], acc_ref[1] / l_ref[1])
    o_ref[...] = out.astype(o_ref.dtype)


def _attn_b(qh, kh, vh, ck, *, tq):
    t = qh.shape[1]
    pair0 = N_HEADS_A // 2
    pairs = N_HEADS_B // 2
    kv_spec = pl.BlockSpec((1, t, LANE), lambda p, i: (pair0 + p, 0, 0))
    return pl.pallas_call(
        functools.partial(_attn_b_kernel, tq=tq, pair0=pair0),
        grid=(pairs, t // tq),
        in_specs=[pl.BlockSpec((1, tq, LANE), lambda p, i: (pair0 + p, i, 0)), kv_spec, kv_spec,
                  pl.BlockSpec(ck.shape, lambda p, i: (0, 0))],
        out_specs=pl.BlockSpec((tq, LANE), lambda p, i: (i, p)),
        out_shape=jax.ShapeDtypeStruct((t, pairs * LANE), BF16),
        scratch_shapes=[pltpu.VMEM((2, tq, LANE), F32), pltpu.VMEM((2, tq, LANE), F32),
                        pltpu.VMEM((2, tq, LANE), F32), pltpu.VMEM((2, SUBLANE, LANE), F32)],
        compiler_params=_cparams("arbitrary", "arbitrary"),
    )(qh, kh, vh, ck)


def _topk_lanes(gate, blk, n_valid_rounds, n_cols):
    sel = jnp.zeros(gate.shape, jnp.bool_)
    for r in range(MOBA_TOPK):
        mx = jnp.max(gate, axis=1, keepdims=True)
        idx = jnp.min(jnp.where(gate == mx, blk, n_cols), axis=1, keepdims=True)
        hit = blk == idx
        sel = jnp.logical_or(sel, jnp.logical_and(hit, r < n_valid_rounds))
        gate = jnp.where(hit, -jnp.inf, gate)
    return sel


def _attn_c_kernel(q_ref, qf_ref, kma_ref, kmb_ref, k_ref, v_ref, o_ref, acc_ref, m_ref, l_ref,
                   kmax_ref, *, tq, nb):
    own = pl.program_id(1)

    @pl.when(own == 0)
    def _():
        _pair_key_norm_max(k_ref, kmax_ref, tq)

    lane = _iota((tq, LANE), 1)
    left = lane < HEAD_DIM
    q = q_ref[0]
    zero = jnp.zeros_like(q)
    qs = (jnp.where(left, q, zero), jnp.where(left, zero, q))
    qf = qf_ref[...]
    qfs = (jnp.where(left, qf, 0.0), jnp.where(left, 0.0, qf))
    lane0 = (HEAD_DIM, 0)
    q_aug = []
    for h in range(2):
        gate = _dot_nt_f32(qfs[h], (kma_ref, kmb_ref)[h][...])
        blk = lane - lane0[h]
        valid = jnp.logical_and(blk >= 0, blk < own)
        sel = _topk_lanes(jnp.where(valid, gate, NEG), jnp.where(blk >= 0, blk, LANE), own, LANE)
        in_range = jnp.logical_and(blk >= 0, blk < nb)
        drop = jnp.where(jnp.logical_and(in_range, jnp.logical_not(sel)), -1.0, 0.0).astype(BF16)
        q_aug.append(qs[h] + drop)
    start = pl.multiple_of(own * tq, tq)
    kb = k_ref[0, pl.ds(start, tq), :]
    vb = v_ref[0, pl.ds(start, tq), :]
    for h in range(2):
        for r0 in range(0, tq, ROW_CHUNK):
            s = jnp.where(_causal_rows(r0, ROW_CHUNK, tq), _dot_nt(qs[h][r0:r0 + ROW_CHUNK], kb), NEG)
            _softmax_step(acc_ref, m_ref, l_ref, h, r0, s, vb, True)

    lane_row = _iota((1, LANE), 1)

    def aug_keys(n, kn, h):
        flag = jnp.where(lane_row == lane0[h] + n, -NEG, 0.0).astype(BF16)
        return kn * _head_lanes(h) + flag

    qn = _pair_query_norms(q, tq)
    gap = [jnp.max(qn[h] * kmax_ref[h][:1, :1] - m_ref[h][:, :1]) for h in range(2)]
    fast = jnp.maximum(gap[0], gap[1]) <= SHIFT_MARGIN

    @pl.when(fast)
    def _():
        _fixed_shift_begin(acc_ref, l_ref, tq)

        def one(n):
            st = pl.multiple_of(n * tq, tq)
            kn = k_ref[0, pl.ds(st, tq), :]
            vn = v_ref[0, pl.ds(st, tq), :]
            for h in range(2):
                _fixed_shift_step(acc_ref, m_ref, h, _dot_nt(q_aug[h], aug_keys(n, kn, h)), vn)

        def body(n4, carry):
            for u in range(4):
                one(4 * n4 + u)
            return carry

        lax.fori_loop(0, own // 4, body, 0)
        base = (own // 4) * 4

        @pl.when(own % 4 >= 2)
        def _():
            one(base)
            one(base + 1)

        @pl.when(own % 2 == 1)
        def _():
            one(own - 1)

        _fixed_shift_end(acc_ref, l_ref, tq)

    @pl.when(jnp.logical_not(fast))
    def _():
        def body(n, carry):
            st = pl.multiple_of(n * tq, tq)
            kn = k_ref[0, pl.ds(st, tq), :]
            vn = v_ref[0, pl.ds(st, tq), :]
            for h in range(2):
                k_aug = aug_keys(n, kn, h)
                for r0 in range(0, tq, ROW_CHUNK):
                    s = _dot_nt(q_aug[h][r0:r0 + ROW_CHUNK], k_aug)
                    _softmax_step(acc_ref, m_ref, l_ref, h, r0, s, vn, False)
            return carry

        lax.fori_loop(0, own, body, 0)

    out = jnp.where(left, acc_ref[0] / l_ref[0], acc_ref[1] / l_ref[1])
    o_ref[...] = out.astype(o_ref.dtype)


def _attn_c(qh, qc, kmean, kh, vh):
    t = qh.shape[1]
    tq = MOBA_BLOCK
    nb = t // tq
    assert nb <= HEAD_DIM, "gate lanes must fit beside one head"
    pair0 = (N_HEADS_A + N_HEADS_B) // 2
    pairs = N_HEADS_C // 2
    km_a = jnp.zeros((LANE, C_W), F32).at[HEAD_DIM:HEAD_DIM + nb].set(kmean)
    km_b = jnp.zeros((LANE, C_W), F32).at[:nb].set(kmean)
    kv_spec = pl.BlockSpec((1, t, LANE), lambda p, i: (pair0 + p, 0, 0))
    km_spec = pl.BlockSpec((LANE, LANE), lambda p, i: (0, p))
    return pl.pallas_call(
        functools.partial(_attn_c_kernel, tq=tq, nb=nb),
        grid=(pairs, nb),
        in_specs=[pl.BlockSpec((1, tq, LANE), lambda p, i: (pair0 + p, i, 0)),
                  pl.BlockSpec((tq, LANE), lambda p, i: (i, p)), km_spec, km_spec, kv_spec, kv_spec],
        out_specs=pl.BlockSpec((tq, LANE), lambda p, i: (i, p)),
        out_shape=jax.ShapeDtypeStruct((t, C_W), BF16),
        scratch_shapes=[pltpu.VMEM((2, tq, LANE), F32)] * 3 + [pltpu.VMEM((2, SUBLANE, LANE), F32)],
        compiler_params=_cparams("arbitrary", "arbitrary"),
    )(qh, qc, km_a, km_b, kh, vh)


def _merge_kernel(x_ref, ya_ref, yb_ref, yc_ref, gate_ref, wa_ref, wb_ref, wc_ref, wo_ref,
                  g1_ref, o_ref):
    m = (gate_ref[:, :D_MODEL] * _dot(ya_ref[...], wa_ref[...])
         + gate_ref[:, D_MODEL:2 * D_MODEL] * _dot(yb_ref[...], wb_ref[...])
         + gate_ref[:, 2 * D_MODEL:] * _dot(yc_ref[...], wc_ref[...]))
    o_ref[...] = x_ref[...] + g1_ref[...] * _dot(m.astype(BF16), wo_ref[...])


def _merge(x, ya, yb, yc, gates, wa, wb, wc, wo, g1, *, tm):
    t = x.shape[0]
    per_row = g1.shape[0] != 1
    mod_spec = (pl.BlockSpec((tm, D_MODEL), lambda i: (i, 0)) if per_row
                else pl.BlockSpec((1, D_MODEL), lambda i: (0, 0)))

    def rows(w):
        return pl.BlockSpec((tm, w), lambda i: (i, 0))

    return pl.pallas_call(
        _merge_kernel,
        grid=(t // tm,),
        in_specs=[rows(D_MODEL), rows(ya.shape[1]), rows(yb.shape[1]), rows(yc.shape[1]),
                  rows(3 * D_MODEL), _const_spec(wa.shape), _const_spec(wb.shape),
                  _const_spec(wc.shape), _const_spec(wo.shape), mod_spec],
        out_specs=rows(D_MODEL),
        out_shape=jax.ShapeDtypeStruct((t, D_MODEL), F32),
        compiler_params=_cparams("arbitrary"),
    )(x, ya, yb, yc, gates, wa, wb, wc, wo, g1)


def _silu(g):
    return g * jax.nn.sigmoid(g)


def _finish(x, acc, g2, fin_ref):
    y = x + g2 * acc
    if fin_ref is not None:
        y = _rmsnorm(y, fin_ref[...])
    return y


def _ffn_kernel(x_ref, sh_ref, sc_ref, g2_ref, ng_ref, wg_ref, wu_ref, wd_ref, *rest,
                chunk, final):
    fin_ref = rest[0] if final else None
    o_ref = rest[-1]
    x = x_ref[...]
    hb = (_rmsnorm(x, ng_ref[...]) * (1.0 + sc_ref[...]) + sh_ref[...]).astype(BF16)
    acc = None
    for c in range(wg_ref.shape[1] // chunk):
        cols = slice(c * chunk, (c + 1) * chunk)
        a = (_silu(_dot(hb, wg_ref[:, cols])) * _dot(hb, wu_ref[:, cols])).astype(BF16)
        part = _dot(a, wd_ref[cols, :])
        acc = part if acc is None else acc + part
    o_ref[...] = _finish(x, acc, g2_ref[...], fin_ref)


def _ffn(x, sh, sc, g2, ng, wg, wu, wd, final_g, *, tm):
    t = x.shape[0]
    per_row = sh.shape[0] != 1
    mod_spec = (pl.BlockSpec((tm, D_MODEL), lambda i: (i, 0)) if per_row
                else pl.BlockSpec((1, D_MODEL), lambda i: (0, 0)))
    rows = pl.BlockSpec((tm, D_MODEL), lambda i: (i, 0))
    final = final_g is not None
    args = [x, sh, sc, g2, ng, wg, wu, wd] + ([final_g] if final else [])
    in_specs = [rows, mod_spec, mod_spec, mod_spec, _const_spec((1, D_MODEL)),
                _const_spec(wg.shape), _const_spec(wu.shape), _const_spec(wd.shape)]
    if final:
        in_specs.append(_const_spec((1, D_MODEL)))
    return pl.pallas_call(
        functools.partial(_ffn_kernel, chunk=256, final=final),
        grid=(t // tm,),
        in_specs=in_specs,
        out_specs=rows,
        out_shape=jax.ShapeDtypeStruct((t, D_MODEL), F32),
        compiler_params=_cparams("arbitrary"),
    )(*args)


def _moe_kernel(x_ref, sh_ref, sc_ref, g2_ref, ng_ref, rw_ref, rb_ref, wg_ref, wu_ref, wd_ref,
                *rest, tm, final):
    fin_ref = rest[0] if final else None
    o_ref, hb_ref, gate_ref, acc_ref = rest[-4:]
    e = pl.program_id(1)
    c = pl.program_id(2)
    first = jnp.logical_and(e == 0, c == 0)
    last = jnp.logical_and(e == pl.num_programs(1) - 1, c == pl.num_programs(2) - 1)

    @pl.when(first)
    def _():
        h = _rmsnorm(x_ref[...], ng_ref[...]) * (1.0 + sc_ref[...]) + sh_ref[...]
        hb_ref[...] = h.astype(BF16)
        h0, h1, h2 = _split3(h)
        w0, w1, w2 = _split3(rw_ref[...])
        logits = (_dot(h0, w0) + _dot(h0, w1) + _dot(h1, w0) + _dot(h1, w1) + _dot(h0, w2)
                  + _dot(h2, w0) + rb_ref[...])
        col = _iota((tm, LANE), 1)
        logits = jnp.where(col < N_EXPERTS, logits, -jnp.inf)
        v1 = jnp.max(logits, axis=1, keepdims=True)
        i1 = jnp.min(jnp.where(logits == v1, col, LANE), axis=1, keepdims=True)
        rest_l = jnp.where(col == i1, -jnp.inf, logits)
        v2 = jnp.max(rest_l, axis=1, keepdims=True)
        i2 = jnp.min(jnp.where(rest_l == v2, col, LANE), axis=1, keepdims=True)
        e2 = jnp.exp(v2 - v1)
        w1st = 1.0 / (1.0 + e2)
        w2nd = e2 / (1.0 + e2)
        for ex in range(N_EXPERTS):
            gate = jnp.where(i1 == ex, w1st, 0.0) + jnp.where(i2 == ex, w2nd, 0.0)
            gate_ref[ex] = jnp.broadcast_to(gate, (tm, LANE))
        acc_ref[...] = jnp.zeros_like(acc_ref)

    hb = hb_ref[...]
    a = (_silu(_dot(hb, wg_ref[0])) * _dot(hb, wu_ref[0])).astype(BF16)
    y = _dot(a, wd_ref[0])
    gate = gate_ref[e]
    for s in range(D_MODEL // LANE):
        acc_ref[:, s * LANE:(s + 1) * LANE] += gate * y[:, s * LANE:(s + 1) * LANE]

    @pl.when(last)
    def _():
        o_ref[...] = _finish(x_ref[...], acc_ref[...], g2_ref[...], fin_ref)


def _moe(x, sh, sc, g2, ng, rw, rb, wg, wu, wd, final_g, *, tm, tf):
    t = x.shape[0]
    n_exp, _, d_exp = wg.shape
    per_row = sh.shape[0] != 1
    mod_spec = (pl.BlockSpec((tm, D_MODEL), lambda i, e, c: (i, 0)) if per_row
                else pl.BlockSpec((1, D_MODEL), lambda i, e, c: (0, 0)))
    rows = pl.BlockSpec((tm, D_MODEL), lambda i, e, c: (i, 0))
    final = final_g is not None
    args = [x, sh, sc, g2, ng, rw, rb, wg, wu, wd] + ([final_g] if final else [])
    in_specs = [rows, mod_spec, mod_spec, mod_spec, _const_spec((1, D_MODEL)),
                _const_spec(rw.shape), _const_spec(rb.shape),
                pl.BlockSpec((1, D_MODEL, tf), lambda i, e, c: (e, 0, c)),
                pl.BlockSpec((1, D_MODEL, tf), lambda i, e, c: (e, 0, c)),
                pl.BlockSpec((1, tf, D_MODEL), lambda i, e, c: (e, c, 0))]
    if final:
        in_specs.append(_const_spec((1, D_MODEL)))
    return pl.pallas_call(
        functools.partial(_moe_kernel, tm=tm, final=final),
        grid=(t // tm, n_exp, d_exp // tf),
        in_specs=in_specs,
        out_specs=rows,
        out_shape=jax.ShapeDtypeStruct((t, D_MODEL), F32),
        scratch_shapes=[pltpu.VMEM((tm, D_MODEL), BF16), pltpu.VMEM((N_EXPERTS, tm, LANE), F32),
                        pltpu.VMEM((tm, D_MODEL), F32)],
        compiler_params=_cparams("arbitrary", "arbitrary", "arbitrary"),
    )(*args)


ROWS_PER_HEAD = SUBLANE
AB_ROWS = (N_HEADS_A + N_HEADS_B) * ROWS_PER_HEAD
A_ROWS = N_HEADS_A * ROWS_PER_HEAD
HEADS_PER_GROUP = SUBLANE
PAGES_PER_STEP = 8


def _sample_ab_kernel(pt_ref, q_ref, kn_ref, vn_ref, lfn_ref, *rest, n_new, n_pg):
    del pt_ref
    k_refs, kc_refs = rest[:n_pg], rest[n_pg:2 * n_pg]
    v_refs, lf_refs = rest[2 * n_pg:3 * n_pg], rest[3 * n_pg:4 * n_pg]
    o_ref, ps_ref, acc_ref, run_ref, m_ref, l_ref, carry_ref = rest[4 * n_pg:]
    pstep = pl.program_id(1)
    n_steps = pl.num_programs(1)
    page = k_refs[0].shape[4]
    q = q_ref[0]
    later = (_iota((page, page), 0) > _iota((page, page), 1)).astype(BF16)

    def update(kts, vts, bias8s, mask_a, mask_b):
        n = len(kts)
        s = _dot(q, jnp.concatenate(kts, axis=1) if n > 1 else kts[0])
        lb, l1m = _log_sigmoid_pair(s[:A_ROWS])
        if mask_a is not None:
            l1m = jnp.where(mask_a, l1m, 0.0)
        run = run_ref[...][:, :1]
        ws = []
        for j in range(n):
            lj = l1m[:, j * page:(j + 1) * page]
            hi = lj.astype(BF16)
            lo = (lj - hi.astype(F32)).astype(BF16)
            after = _dot(hi, later) + _dot(lo, later) + run
            ws.append(jnp.exp(lb[:, j * page:(j + 1) * page] + after))
            run = run + jnp.sum(lj, axis=1, keepdims=True)
        run_ref[...] = jnp.broadcast_to(run, run_ref.shape)
        w = jnp.concatenate(ws, axis=1) if n > 1 else ws[0]
        if mask_a is not None:
            w = jnp.where(mask_a, w, 0.0)
        bias = jnp.concatenate(
            [jnp.concatenate([jnp.broadcast_to(b8[h:h + 1], (ROWS_PER_HEAD, page))
                              for h in range(N_HEADS_B)], axis=0) for b8 in bias8s], axis=1)
        lg = s[A_ROWS:] + bias
        if mask_b is not None:
            lg = jnp.where(mask_b, lg, NEG)
        m_old = m_ref[...][:, :1]
        m_new = jnp.maximum(m_old, jnp.max(lg, axis=1, keepdims=True))
        alpha = jnp.exp(m_old - m_new)
        pw = jnp.exp(lg - m_new)
        l_ref[...] = alpha * l_ref[...] + jnp.sum(pw, axis=1, keepdims=True)
        m_ref[...] = jnp.broadcast_to(m_new, m_ref.shape)
        scale = jnp.concatenate([jnp.ones((A_ROWS, 1), F32), alpha], axis=0)
        weights = jnp.concatenate([w, pw], axis=0).astype(BF16)
        pv = _dot_nt(weights, jnp.concatenate(vts, axis=1) if n > 1 else vts[0])
        acc_ref[...] = scale * acc_ref[...] + pv

    @pl.when(pstep == 0)
    def _():
        acc_ref[...] = jnp.zeros_like(acc_ref)
        run_ref[...] = jnp.zeros_like(run_ref)
        m_ref[...] = jnp.full(m_ref.shape, NEG, F32)
        l_ref[...] = jnp.zeros_like(l_ref)
        carry_ref[...] = jnp.zeros_like(carry_ref)
        upto = (_iota((page, page), 0) <= _iota((page, page), 1)).astype(BF16)
        cum_new = _dot_exact_rhs(lfn_ref[0], upto)
        qa = _iota((A_ROWS, page), 0) % ROWS_PER_HEAD
        la = _iota((A_ROWS, page), 1)
        qb = _iota((AB_ROWS - A_ROWS, page), 0) % ROWS_PER_HEAD
        lb_ = _iota((AB_ROWS - A_ROWS, page), 1)
        mask_a = jnp.logical_and(la < qa, la < n_new)
        mask_b = jnp.logical_and(lb_ <= qb, lb_ < n_new)
        update([kn_ref[0]], [vn_ref[0]], [-cum_new], mask_a, mask_b)

    def tile(ref):
        return ref[0, 0].reshape(HEADS_PER_GROUP * HEAD_DIM, page)

    carry = carry_ref[...][:, :1]
    bias8s = []
    for j in range(n_pg):
        lf8 = lf_refs[j][0, 0]
        bias8s.append(_dot_exact_rhs(lf8, later) + carry)
        carry = carry + jnp.sum(lf8, axis=1, keepdims=True)
    carry_ref[...] = jnp.broadcast_to(carry, carry_ref.shape)
    update([tile(r).astype(BF16) for r in k_refs], [tile(r).astype(BF16) for r in v_refs],
           bias8s, None, None)
    ones = jnp.ones((SUBLANE, page), BF16)
    newest = n_pg * (n_steps - pstep) - 1
    for j in range(n_pg):
        k0, k1, k2 = _split3(tile(kc_refs[j]))
        sums = _dot_nt(ones, k0) + _dot_nt(ones, k1) + _dot_nt(ones, k2)
        ps_ref[0, pl.ds(newest - j, 1), :] = sums[:1]

    @pl.when(pstep == n_steps - 1)
    def _():
        l_full = jnp.concatenate([jnp.ones((A_ROWS, 1), F32), l_ref[...][:, :1]], axis=0)
        y = acc_ref[...] / l_full
        r2 = _iota((AB_ROWS, AB_W), 0) // ROWS_PER_HEAD
        c2 = _iota((AB_ROWS, AB_W), 1) // HEAD_DIM
        y = jnp.where(r2 == c2, y, 0.0)
        o_ref[0] = jnp.sum(y.reshape(AB_ROWS // ROWS_PER_HEAD, ROWS_PER_HEAD, AB_W), axis=0)


def _sample_ab(layer, page_table, q_bd, k_new, v_new, lf_new, cache_kt, cache_vt, cache_lf, *, n_new):
    n_b, n_pages = page_table.shape
    page = cache_kt.shape[4]
    n_pg = min(PAGES_PER_STEP, n_pages)
    assert n_pages % n_pg == 0

    def page_idx(b, p, pt, j):
        return pt[b, n_pages - 1 - (n_pg * p + j)]

    def group_spec(group, j):
        return pl.BlockSpec((1, 1, HEADS_PER_GROUP, HEAD_DIM, page),
                            lambda b, p, pt: (layer, page_idx(b, p, pt, j), group, 0, 0))

    def lf_spec(j):
        return pl.BlockSpec((1, 1, SUBLANE, page),
                            lambda b, p, pt: (layer, page_idx(b, p, pt, j), 0, 0))

    pages = range(n_pg)
    grid_spec = pltpu.PrefetchScalarGridSpec(
        num_scalar_prefetch=1,
        grid=(n_b, n_pages // n_pg),
        in_specs=[
            pl.BlockSpec((1, AB_ROWS, AB_W), lambda b, p, pt: (b, 0, 0)),
            pl.BlockSpec((1, AB_W, page), lambda b, p, pt: (b, 0, 0)),
            pl.BlockSpec((1, AB_W, page), lambda b, p, pt: (b, 0, 0)),
            pl.BlockSpec((1, SUBLANE, page), lambda b, p, pt: (b, 0, 0)),
        ] + [group_spec(0, j) for j in pages] + [group_spec(1, j) for j in pages]
          + [group_spec(0, j) for j in pages] + [lf_spec(j) for j in pages],
        out_specs=[
            pl.BlockSpec((1, ROWS_PER_HEAD, AB_W), lambda b, p, pt: (b, 0, 0)),
            pl.BlockSpec((1, n_pages, C_W), lambda b, p, pt: (b, 0, 0)),
        ],
        scratch_shapes=[
            pltpu.VMEM((AB_ROWS, AB_W), F32),
            pltpu.VMEM((A_ROWS, LANE), F32),
            pltpu.VMEM((AB_ROWS - A_ROWS, LANE), F32),
            pltpu.VMEM((AB_ROWS - A_ROWS, LANE), F32),
            pltpu.VMEM((SUBLANE, LANE), F32),
        ],
    )
    return pl.pallas_call(
        functools.partial(_sample_ab_kernel, n_new=n_new, n_pg=n_pg),
        grid_spec=grid_spec,
        out_shape=[jax.ShapeDtypeStruct((n_b, ROWS_PER_HEAD, AB_W), F32),
                   jax.ShapeDtypeStruct((n_b, n_pages, C_W), F32)],
        compiler_params=_cparams("arbitrary", "arbitrary"),
    )(page_table, q_bd, k_new, v_new, lf_new, *([cache_kt] * (2 * n_pg)), *([cache_vt] * n_pg),
      *([cache_lf] * n_pg))


C_ROWS = N_HEADS_C * ROWS_PER_HEAD
Q_ROWS_BF16 = 2 * SUBLANE


def _sample_gate_kernel(ps_ref, q_ref, o_ref, *, n_blk, pages_per_blk):
    n_pages = ps_ref.shape[1]
    pair = (_iota((n_blk, n_pages), 1) // pages_per_blk == _iota((n_blk, n_pages), 0)).astype(BF16)
    kmean = _dot_exact_lhs(pair, ps_ref[0]) * (1.0 / MOBA_BLOCK)
    gate = _dot_nt_f32(q_ref[0], kmean)
    blk = _iota(gate.shape, 1)
    picks = []
    for _ in range(MOBA_TOPK):
        mx = jnp.max(gate, axis=1, keepdims=True)
        idx = jnp.min(jnp.where(gate == mx, blk, n_blk), axis=1, keepdims=True)
        picks.append(idx)
        gate = jnp.where(blk == idx, -jnp.inf, gate)
    lane = _iota((C_ROWS, LANE), 1)
    out = jnp.zeros((C_ROWS, LANE), jnp.int32)
    for r, idx in enumerate(picks):
        out = jnp.where(lane == r, idx, out)
    o_ref[0] = out


def _sample_gate(page_sums, qc_bd, *, n_blk, pages_per_blk):
    n_b, n_pages, _ = page_sums.shape
    return pl.pallas_call(
        functools.partial(_sample_gate_kernel, n_blk=n_blk, pages_per_blk=pages_per_blk),
        grid=(n_b,),
        in_specs=[pl.BlockSpec((1, n_pages, C_W), lambda b: (b, 0, 0)),
                  pl.BlockSpec((1, C_ROWS, C_W), lambda b: (b, 0, 0))],
        out_specs=pl.BlockSpec((1, C_ROWS, LANE), lambda b: (b, 0, 0)),
        out_shape=jax.ShapeDtypeStruct((n_b, C_ROWS, LANE), jnp.int32),
        compiler_params=_cparams("arbitrary"),
    )(page_sums, qc_bd)


def _sample_c_kernel(sel_ref, pt_ref, q_ref, kn_ref, vn_ref, *rest, n_new, n_q, pages_per_blk):
    del sel_ref, pt_ref
    per_head = n_q * MOBA_TOPK * pages_per_blk
    k_refs, v_refs = rest[:2 * per_head], rest[2 * per_head:4 * per_head]
    o_ref = rest[4 * per_head]
    page = kn_ref.shape[3]
    n_cols = (1 + per_head) * page
    row = _iota((Q_ROWS_BF16, n_cols), 0)
    col = _iota((Q_ROWS_BF16, n_cols), 1)
    owner = (col // page - 1) // (MOBA_TOPK * pages_per_blk)
    fresh = col < page
    live = jnp.logical_or(
        jnp.logical_and(fresh, jnp.logical_and(col <= row, col < n_new)),
        jnp.logical_and(jnp.logical_not(fresh), owner == row))
    for h in range(2):
        tiles = slice(h * per_head, (h + 1) * per_head)
        kt = jnp.concatenate([kn_ref[0, h]] + [r[0, 0, 0].astype(BF16) for r in k_refs[tiles]], axis=1)
        vt = jnp.concatenate([vn_ref[0, h]] + [r[0, 0, 0].astype(BF16) for r in v_refs[tiles]], axis=1)
        s = jnp.where(live, _dot(q_ref[0, h], kt), NEG)
        m = jnp.max(s, axis=1, keepdims=True)
        pw = jnp.exp(s - m)
        l = jnp.sum(pw, axis=1, keepdims=True)
        o_ref[0, h] = _dot_nt(pw.astype(BF16), vt) / l


def _sample_c(layer, sel_flat, pt_flat, q8, k_new, v_new, cache_kt, cache_vt, *,
              n_b, n_pages, n_new, n_q, pages_per_blk):
    page = cache_kt.shape[4]
    pairs = N_HEADS_C // 2

    def fetch_spec(h, qi, r, half):
        def index(b, p, sel, pt):
            head = 2 * p + h
            blk = sel[((b * N_HEADS_C + head) * n_q + qi) * MOBA_TOPK + r]
            return (layer, pt[b * n_pages + blk * pages_per_blk + half],
                    N_HEADS_A + N_HEADS_B + head, 0, 0)
        return pl.BlockSpec((1, 1, 1, HEAD_DIM, page), index)

    fetch = [fetch_spec(h, qi, r, half) for h in range(2) for qi in range(n_q)
             for r in range(MOBA_TOPK) for half in range(pages_per_blk)]
    q_spec = pl.BlockSpec((1, 2, Q_ROWS_BF16, HEAD_DIM), lambda b, p, sel, pt: (b, p, 0, 0))
    new_spec = pl.BlockSpec((1, 2, HEAD_DIM, page), lambda b, p, sel, pt: (b, p, 0, 0))
    grid_spec = pltpu.PrefetchScalarGridSpec(
        num_scalar_prefetch=2,
        grid=(n_b, pairs),
        in_specs=[q_spec, new_spec, new_spec] + fetch + fetch,
        out_specs=pl.BlockSpec((1, 2, Q_ROWS_BF16, HEAD_DIM), lambda b, p, sel, pt: (b, p, 0, 0)),
    )
    return pl.pallas_call(
        functools.partial(_sample_c_kernel, n_new=n_new, n_q=n_q, pages_per_blk=pages_per_blk),
        grid_spec=grid_spec,
        out_shape=jax.ShapeDtypeStruct((n_b, N_HEADS_C, Q_ROWS_BF16, HEAD_DIM), F32),
        compiler_params=_cparams("arbitrary", "arbitrary"),
    )(sel_flat, pt_flat, q8, k_new, v_new, *([cache_kt] * len(fetch)), *([cache_vt] * len(fetch)))


def _pad_rows(a, rows, axis):
    pad = [(0, 0)] * a.ndim
    pad[axis] = (0, rows - a.shape[axis])
    return jnp.pad(a, pad)


def _block_diag_rows(q, n_heads):
    n_b, n_s, w = q.shape
    q8 = _pad_rows(q, ROWS_PER_HEAD, 1)
    tiled = jnp.tile(q8, (1, n_heads, 1))
    keep = (jnp.arange(n_heads * ROWS_PER_HEAD)[:, None] // ROWS_PER_HEAD
            == jnp.arange(w)[None, :] // HEAD_DIM)
    return jnp.where(keep[None], tiled, jnp.zeros((), q.dtype))


def _layer_weights(l, w_in, b_f, w_br_a, w_br_b, w_br_c, w_out):
    wi = w_in[l]
    f0 = 3 * QKV_W
    wqkv = wi[:, :f0].astype(BF16)
    wf = _pad_rows(wi[:, f0:f0 + N_HEADS_B].T, SUBLANE, 0).astype(BF16)
    bf = _pad_rows(b_f[l].reshape(N_HEADS_B, 1), SUBLANE, 0)
    wg = wi[:, f0 + N_HEADS_B:].astype(BF16)
    return (wqkv, wf, bf, wg, w_br_a[l].astype(BF16), w_br_b[l].astype(BF16),
            w_br_c[l].astype(BF16), w_out[l].astype(BF16))


def kernel(x_prompt, x_sample, c_prompt, c_sample, cache_k, cache_v, cache_logf, page_table,
           ada_w, ada_b, norm1_g, norm2_g, w_in, b_f, w_br_a, w_br_b, w_br_c, w_out,
           ffn_w_gate, ffn_w_up, ffn_w_down, moe_router_w, moe_router_b, moe_w_gate,
           moe_w_up, moe_w_down, final_g):
    depth = ada_w.shape[0]
    n_bp, t_p, _ = x_prompt.shape
    assert n_bp == 1, "one prompt sequence"
    n_b, n_s, _ = x_sample.shape
    n_phys, page = cache_k.shape[1], cache_k.shape[2]
    n_pages = page_table.shape[1]
    p_len = n_pages * page
    pages_per_blk = MOBA_BLOCK // page
    n_blk_s = p_len // MOBA_BLOCK
    assert p_len % MOBA_BLOCK == 0 and n_s <= ROWS_PER_HEAD and n_blk_s >= MOBA_TOPK
    t_s = n_b * n_s

    tm_p = min(512, t_p)
    tq = MOBA_BLOCK
    tm_s = t_s

    c_rows = _pad_rows(jnp.concatenate([c_prompt, c_sample], axis=0),
                       -(-(n_bp + n_b) // SUBLANE) * SUBLANE, 0)
    mod = _modulation(c_rows, ada_w, ada_b)

    def mod_parts(l):
        parts_p = [mod[l, :1, j * D_MODEL:(j + 1) * D_MODEL] for j in range(6)]
        parts_s = [jnp.repeat(mod[l, 1:1 + n_b, j * D_MODEL:(j + 1) * D_MODEL], n_s, axis=0)
                   for j in range(6)]
        return parts_p, parts_s

    tabs_p = _rope_tables(jnp.arange(t_p))
    tabs_s = _rope_tables(jnp.tile(p_len + jnp.arange(n_s), n_b))

    cache_kt = jnp.transpose(cache_k, (0, 1, 3, 4, 2))
    cache_vt = jnp.transpose(cache_v, (0, 1, 3, 4, 2))
    cache_lf4 = _pad_rows(jnp.swapaxes(cache_logf, 2, 3), SUBLANE, 2)
    pt_flat = page_table.reshape(-1)

    xp = x_prompt.reshape(t_p, D_MODEL)
    xs = x_sample.reshape(t_s, D_MODEL)
    outs = {k: [] for k in ("kp", "vp", "lp", "ks", "vs", "ls")}
    for l in range(depth):
        wqkv, wf, bf, wg, wa, wb, wc, wo = _layer_weights(l, w_in, b_f, w_br_a, w_br_b, w_br_c, w_out)
        (sh1p, sc1p, g1p, sh2p, sc2p, g2p), (sh1s, sc1s, g1s, sh2s, sc2s, g2s) = mod_parts(l)
        n1 = norm1_g[l].reshape(1, D_MODEL)
        n2 = norm2_g[l].reshape(1, D_MODEL)
        last = l == depth - 1
        fin = final_g.reshape(1, D_MODEL) if last else None

        qh, kh, vh, k_rows, v_rows, lf_rows, gates, qc, kmean = _inproj(
            xp, sh1p, sc1p, n1, wqkv, wf, bf, wg, *tabs_p, tm=tm_p, with_kmean=True)
        ck = _seq_cumsum(lf_rows)
        ya = _attn_a(qh, kh, vh, tq=tq)
        yb = _attn_b(qh, kh, vh, ck, tq=tq)
        yc = _attn_c(qh, qc, kmean.reshape(t_p // MOBA_BLOCK, C_W), kh, vh)
        xp = _merge(xp, ya, yb, yc, gates, wa, wb, wc, wo, g1p, tm=tm_p)
        outs["kp"].append(k_rows)
        outs["vp"].append(v_rows)
        outs["lp"].append(lf_rows[:N_HEADS_B].T)

        qh_s, kh_s, vh_s, k_rows_s, v_rows_s, lf_rows_s, gates_s, qc_s = _inproj(
            xs, sh1s, sc1s, n1, wqkv, wf, bf, wg, *tabs_s, tm=tm_s, with_kmean=False)
        def heads_t(rows, lo, hi):
            a4 = rows.reshape(n_b, n_s, N_HEADS, HEAD_DIM)[:, :, lo:hi]
            return _pad_rows(jnp.transpose(a4, (0, 2, 3, 1)), page, 3).astype(BF16)

        def queries(lo, hi):
            q_cols = jnp.moveaxis(qh_s, 0, 1).reshape(n_b, n_s, N_HEADS, HEAD_DIM)
            return q_cols[:, :, lo:hi]

        n_ab = N_HEADS_A + N_HEADS_B
        q_bd = _block_diag_rows(queries(0, n_ab).reshape(n_b, n_s, AB_W), n_ab)
        lf_new = _pad_rows(_pad_rows(
            jnp.swapaxes(lf_rows_s[:N_HEADS_B].T.reshape(n_b, n_s, N_HEADS_B), 1, 2), SUBLANE, 1), page, 2)
        yab_s, page_sums = _sample_ab(
            l, page_table, q_bd, heads_t(k_rows_s, 0, n_ab).reshape(n_b, AB_W, page),
            heads_t(v_rows_s, 0, n_ab).reshape(n_b, AB_W, page), lf_new,
            cache_kt, cache_vt, cache_lf4, n_new=n_s)
        qc_bd = _block_diag_rows(qc_s.reshape(n_b, n_s, C_W), N_HEADS_C)
        picks = _sample_gate(page_sums, qc_bd, n_blk=n_blk_s, pages_per_blk=pages_per_blk)
        sel = picks.reshape(n_b, N_HEADS_C, ROWS_PER_HEAD, LANE)[:, :, :n_s, :MOBA_TOPK].reshape(-1)
        q8 = _pad_rows(jnp.swapaxes(queries(n_ab, N_HEADS), 1, 2), Q_ROWS_BF16, 2)
        yc_s = _sample_c(l, sel, pt_flat, q8, heads_t(k_rows_s, n_ab, N_HEADS),
                         heads_t(v_rows_s, n_ab, N_HEADS), cache_kt, cache_vt, n_b=n_b,
                         n_pages=n_pages, n_new=n_s, n_q=n_s, pages_per_blk=pages_per_blk)
        ya_s = yab_s[:, :n_s, :N_HEADS_A * HEAD_DIM].reshape(t_s, -1).astype(BF16)
        yb_s = yab_s[:, :n_s, N_HEADS_A * HEAD_DIM:].reshape(t_s, -1).astype(BF16)
        yc_sb = jnp.swapaxes(yc_s[:, :, :n_s], 1, 2).reshape(t_s, C_W).astype(BF16)
        xs = _merge(xs, ya_s, yb_s, yc_sb, gates_s, wa, wb, wc, wo, g1s, tm=tm_s)
        outs["ks"].append(k_rows_s)
        outs["vs"].append(v_rows_s)
        outs["ls"].append(lf_rows_s[:N_HEADS_B].T)

        j = l // 2
        if l % 2 == 0:
            fw = (ffn_w_gate[j].astype(BF16), ffn_w_up[j].astype(BF16), ffn_w_down[j].astype(BF16))
            xp = _ffn(xp, sh2p, sc2p, g2p, n2, *fw, fin, tm=tm_p)
            xs = _ffn(xs, sh2s, sc2s, g2s, n2, *fw, fin, tm=tm_s)
        else:
            rw = jnp.pad(moe_router_w[j], ((0, 0), (0, LANE - N_EXPERTS)))
            rb = jnp.pad(moe_router_b[j], (0, LANE - N_EXPERTS)).reshape(1, LANE)
            mw = (moe_w_gate[j].astype(BF16), moe_w_up[j].astype(BF16), moe_w_down[j].astype(BF16))
            xp = _moe(xp, sh2p, sc2p, g2p, n2, rw, rb, *mw, fin, tm=tm_p, tf=512)
            xs = _moe(xs, sh2s, sc2s, g2s, n2, rw, rb, *mw, fin, tm=tm_s, tf=512)

    def heads(rows, lead):
        return jnp.stack(rows).reshape(depth, *lead, N_HEADS, HEAD_DIM)

    return (xp.reshape(n_bp, t_p, D_MODEL), xs.reshape(n_b, n_s, D_MODEL),
            heads(outs["kp"], (n_bp, t_p)), heads(outs["vp"], (n_bp, t_p)),
            jnp.stack(outs["lp"]).reshape(depth, n_bp, t_p, N_HEADS_B),
            heads(outs["ks"], (n_b, n_s)), heads(outs["vs"], (n_b, n_s)),
            jnp.stack(outs["ls"]).reshape(depth, n_b, n_s, N_HEADS_B))
```
